```python
import math
import jax, jax.numpy as jnp
from jax import lax
import numpy as np

D_MODEL = 4096
BATCH = 4
SEQ = 2048
DEPTH = 4
DEC_BATCH = 128
DEC_SEQ = 8
PAST_LEN = 16384
PAGE_SIZE = 128

N_MIXERS = 2
N_S5_LAYERS = (DEPTH + 1) // 2
N_GDN_LAYERS = DEPTH // 2

S5_GROUP = 16
S5_GROUPS = D_MODEL // S5_GROUP
S5_STATE = 64

GDN_HEAD_K = 128
GDN_HEAD_V = 128
GDN_K_HEADS = D_MODEL // 128
GDN_V_HEADS = 2 * GDN_K_HEADS
GDN_K_DIM = GDN_K_HEADS * GDN_HEAD_K
GDN_V_DIM = GDN_V_HEADS * GDN_HEAD_V
GDN_CONV_DIM = 2 * GDN_K_DIM + GDN_V_DIM
GDN_IN_DIM = GDN_CONV_DIM + GDN_V_DIM + 2 * GDN_V_HEADS
GDN_CONV_W = 4
GDN_CHUNK = 64

N_EXPERTS = 32
TOP_K = 4
D_EXPERT = D_MODEL // 4
SWIGLU_ALPHA = 1.702
SWIGLU_LIMIT = 7.0
MOE_BLOCK = 128

DN_ALPHA = (2 * DEPTH) ** 0.25
DN_BETA = (8 * DEPTH) ** -0.25
LN_EPS = 1e-5
RMS_EPS = 1e-6

kernel_name = "s5_gdn_moe_deepnorm_step"


def layer_norm(x, g, b):
    xf = x.astype(jnp.float32)
    mu = jnp.mean(xf, axis=-1, keepdims=True)
    var = jnp.mean(jnp.square(xf - mu), axis=-1, keepdims=True)
    return ((xf - mu) * lax.rsqrt(var + LN_EPS) * g.astype(jnp.float32) + b.astype(jnp.float32)).astype(x.dtype)


def _s5_combine(e1, e2):
    a1, b1 = e1
    a2, b2 = e2
    return a1 * a2, a2 * b1 + b2


def s5_mixer(x, h0_re, h0_im, a_re, a_im, log_step, b_re, b_im, c_re, c_im, d_skip, w_val, w_gate):
    nb, ns, _ = x.shape
    f32 = jnp.float32
    u = x.astype(f32).reshape(nb, ns, S5_GROUPS, S5_GROUP)
    lam = lax.complex(a_re.astype(f32), a_im.astype(f32))
    step = jnp.exp(log_step.astype(f32))
    lam_bar = jnp.exp(lam * step)
    b_mat = lax.complex(b_re.astype(f32), b_im.astype(f32))
    b_bar = ((lam_bar - 1.0) / lam)[..., None] * b_mat
    c_mat = lax.complex(c_re.astype(f32), c_im.astype(f32))
    bu = jnp.einsum('gpc,bsgc->bsgp', b_bar, u.astype(jnp.complex64))
    h0 = lax.complex(h0_re.astype(f32), h0_im.astype(f32))
    bu = bu.at[:, 0].add(lam_bar * h0)
    a = jnp.broadcast_to(lam_bar, (1, ns) + lam_bar.shape)
    _, h = lax.associative_scan(_s5_combine, (a, bu), axis=1)
    y = jnp.real(jnp.einsum('gcp,bsgp->bsgc', c_mat, h)) + d_skip.astype(f32).reshape(S5_GROUPS, S5_GROUP) * u
    z = jax.nn.gelu(y.reshape(nb, ns, D_MODEL), approximate=False).astype(x.dtype)
    out = jnp.einsum('bsd,de->bse', z, w_val) * jax.nn.sigmoid(jnp.einsum('bsd,de->bse', z, w_gate))
    h_last = h[:, -1]
    return out.astype(x.dtype), jnp.real(h_last), jnp.imag(h_last)


def l2_normalize(x):
    return x * lax.rsqrt(jnp.sum(jnp.square(x), axis=-1, keepdims=True) + 1e-6)


def causal_short_conv(x, buf, w):
    ns = x.shape[1]
    xc = jnp.concatenate([buf.astype(x.dtype), x], axis=1)
    y = sum(xc[:, i:i + ns] * w[i] for i in range(GDN_CONV_W))
    return y, xc[:, ns:]


def chunk_gated_delta_rule(q, k, v, g, beta, s0):
    nb, ns, nh, dk = q.shape
    dv = v.shape[-1]
    pad = (-ns) % GDN_CHUNK
    n_chunks = (ns + pad) // GDN_CHUNK

    def to_chunks(t):
        t = jnp.pad(t, [(0, 0), (0, pad)] + [(0, 0)] * (t.ndim - 2))
        t = t.reshape((nb, n_chunks, GDN_CHUNK) + t.shape[2:])
        return jnp.swapaxes(jnp.moveaxis(t, 3, 2), 0, 1)

    q = to_chunks(q * dk ** -0.5)
    k = to_chunks(k)
    v = to_chunks(v)
    g = to_chunks(g)
    beta = to_chunks(beta)
    gc = jnp.cumsum(g, axis=-1)
    idx = jnp.arange(GDN_CHUNK)
    causal = idx[:, None] >= idx[None, :]
    strict = idx[:, None] > idx[None, :]
    decay = jnp.exp(jnp.where(causal, gc[..., :, None] - gc[..., None, :], -jnp.inf))
    kb = k * beta[..., None]
    vb = v * beta[..., None]
    lmat = jnp.where(strict, jnp.einsum('nbhid,nbhjd->nbhij', kb, k) * decay, 0.0)
    eye = jnp.eye(GDN_CHUNK, dtype=jnp.float32)
    tmat = lax.linalg.triangular_solve(eye + lmat, jnp.broadcast_to(eye, lmat.shape), left_side=True, lower=True)
    u = jnp.einsum('nbhij,nbhjd->nbhid', tmat, vb)
    w = jnp.einsum('nbhij,nbhjd->nbhid', tmat, kb * jnp.exp(gc)[..., None])
    qk = jnp.where(causal, jnp.einsum('nbhid,nbhjd->nbhij', q, k) * decay, 0.0)

    def step(s, inp):
        q_i, k_i, u_i, w_i, qk_i, gc_i = inp
        v_new = u_i - jnp.einsum('bhck,bhkv->bhcv', w_i, s)
        o = jnp.einsum('bhck,bhkv->bhcv', q_i * jnp.exp(gc_i)[..., None], s) + jnp.einsum('bhij,bhjv->bhiv', qk_i, v_new)
        g_last = gc_i[..., -1]
        s = s * jnp.exp(g_last)[..., None, None] + jnp.einsum(
            'bhck,bhcv->bhkv', k_i * jnp.exp(g_last[..., None] - gc_i)[..., None], v_new)
        return s, o

    s_final, o = lax.scan(step, s0, (q, k, u, w, qk, gc))
    o = jnp.moveaxis(jnp.swapaxes(o, 0, 1), 2, 3).reshape(nb, n_chunks * GDN_CHUNK, nh, dv)[:, :ns]
    return o, s_final


def gdn_mixer(x, s0, conv0, w_in, conv_w, a_log, dt_bias, norm_w, w_out):
    nb, ns, _ = x.shape
    f32 = jnp.float32
    proj = jnp.einsum('bsd,de->bse', x, w_in)
    o1 = GDN_CONV_DIM
    o2 = o1 + GDN_V_DIM
    o3 = o2 + GDN_V_HEADS
    qkv, z, b_raw, a_raw = proj[..., :o1], proj[..., o1:o2], proj[..., o2:o3], proj[..., o3:]
    conv_out, new_conv = causal_short_conv(qkv, conv0, conv_w)
    conv_out = jax.nn.silu(conv_out.astype(f32))
    q = conv_out[..., :GDN_K_DIM].reshape(nb, ns, GDN_K_HEADS, GDN_HEAD_K)
    k = conv_out[..., GDN_K_DIM:2 * GDN_K_DIM].reshape(nb, ns, GDN_K_HEADS, GDN_HEAD_K)
    v = conv_out[..., 2 * GDN_K_DIM:].reshape(nb, ns, GDN_V_HEADS, GDN_HEAD_V)
    rep = GDN_V_HEADS // GDN_K_HEADS
    q = jnp.repeat(l2_normalize(q), rep, axis=2)
    k = jnp.repeat(l2_normalize(k), rep, axis=2)
    beta = jax.nn.sigmoid(b_raw.astype(f32))
    g = -jnp.exp(a_log.astype(f32)) * jax.nn.softplus(a_raw.astype(f32) + dt_bias.astype(f32))
    o, s_new = chunk_gated_delta_rule(q, k, v, g, beta, s0.astype(f32))
    o = o * lax.rsqrt(jnp.mean(jnp.square(o), axis=-1, keepdims=True) + RMS_EPS) * norm_w.astype(f32)
    o = o * jax.nn.silu(z.astype(f32).reshape(nb, ns, GDN_V_HEADS, GDN_HEAD_V))
    out = jnp.einsum('bse,ed->bsd', o.reshape(nb, ns, GDN_V_DIM).astype(x.dtype), w_out)
    return out, s_new, new_conv


def clamped_swiglu(gu):
    gate, lin = gu[..., :D_EXPERT], gu[..., D_EXPERT:]
    gate = jnp.minimum(gate, SWIGLU_LIMIT)
    lin = jnp.clip(lin, -SWIGLU_LIMIT, SWIGLU_LIMIT)
    return gate * jax.nn.sigmoid(SWIGLU_ALPHA * gate) * (lin + 1.0)


def moe_ffn(x2d, router_w, router_b, w_gu, b_gu, w_down, b_down):
    f32 = jnp.float32
    n_tok = x2d.shape[0]
    logits = jnp.einsum('td,de->te', x2d.astype(f32), router_w.astype(f32)) + router_b.astype(f32)
    top_val, top_idx = lax.top_k(logits, TOP_K)
    gates = jax.nn.softmax(top_val, axis=-1)
    n_assign = n_tok * TOP_K
    flat_e = top_idx.reshape(-1).astype(jnp.int32)
    order = jnp.argsort(flat_e)
    sorted_e = flat_e[order]
    counts = jnp.zeros((N_EXPERTS,), jnp.int32).at[flat_e].add(1)
    padded = (counts + MOE_BLOCK - 1) // MOE_BLOCK * MOE_BLOCK
    starts = jnp.cumsum(counts) - counts
    pad_ends = jnp.cumsum(padded)
    pad_starts = pad_ends - padded
    dest = pad_starts[sorted_e] + jnp.arange(n_assign, dtype=jnp.int32) - starts[sorted_e]
    n_blocks = -(-n_assign // MOE_BLOCK) + N_EXPERTS
    cap = n_blocks * MOE_BLOCK
    tok_buf = jnp.full((cap,), n_tok, jnp.int32).at[dest].set((order // TOP_K).astype(jnp.int32))
    gate_buf = jnp.zeros((cap,), f32).at[dest].set(gates.reshape(-1)[order])
    block_e = jnp.minimum(jnp.searchsorted(pad_ends, jnp.arange(n_blocks, dtype=jnp.int32) * MOE_BLOCK, side='right'),
                          N_EXPERTS - 1)
    x_pad = jnp.concatenate([x2d, jnp.zeros((1, D_MODEL), x2d.dtype)], axis=0)

    def expert_block(args):
        e, tok, gate = args
        xb = x_pad[tok]
        h = clamped_swiglu(xb @ w_gu[e] + b_gu[e])
        y = h.astype(x2d.dtype) @ w_down[e] + b_down[e]
        return y.astype(f32) * gate[:, None]

    ys = lax.map(expert_block, (block_e, tok_buf.reshape(n_blocks, MOE_BLOCK), gate_buf.reshape(n_blocks, MOE_BLOCK)))
    out = jnp.zeros((n_tok + 1, D_MODEL), f32).at[tok_buf].add(ys.reshape(cap, D_MODEL))
    return out[:n_tok].astype(x2d.dtype)


def setup_inputs(seed: int = 0) -> dict:
    key = jax.random.key(seed)
    k = jax.random.split(key, 40)
    f32 = jnp.float32

    def nrm(i, shape, scale=1.0):
        return jax.random.normal(k[i], shape, f32) * scale

    x_prompt = nrm(0, (BATCH, SEQ, D_MODEL))
    x_sample = nrm(1, (DEC_BATCH, DEC_SEQ, D_MODEL))
    state_s5_re = nrm(2, (N_S5_LAYERS, DEC_BATCH, S5_GROUPS, S5_STATE), 0.1)
    state_s5_im = nrm(3, (N_S5_LAYERS, DEC_BATCH, S5_GROUPS, S5_STATE), 0.1)
    state_gdn = nrm(4, (N_GDN_LAYERS, DEC_BATCH, GDN_V_HEADS, GDN_HEAD_K, GDN_HEAD_V), 0.05)
    state_gdn_conv = nrm(5, (N_GDN_LAYERS, DEC_BATCH, GDN_CONV_W - 1, GDN_CONV_DIM))

    ln_mix_g = 1.0 + nrm(6, (DEPTH, D_MODEL), 0.02)
    ln_mix_b = nrm(7, (DEPTH, D_MODEL), 0.02)
    ln_ffn_g = 1.0 + nrm(8, (DEPTH, D_MODEL), 0.02)
    ln_ffn_b = nrm(9, (DEPTH, D_MODEL), 0.02)

    s5_a_re = -0.5 * jnp.exp(nrm(10, (N_S5_LAYERS, S5_GROUPS, S5_STATE), 0.02))
    s5_a_im = jnp.broadcast_to(jnp.pi * jnp.arange(S5_STATE, dtype=f32), (N_S5_LAYERS, S5_GROUPS, S5_STATE))
    s5_log_step = jax.random.uniform(k[11], (N_S5_LAYERS, S5_GROUPS, S5_STATE), f32,
                                     minval=math.log(1e-3), maxval=math.log(1e-1))
    s5_b_re = nrm(12, (N_S5_LAYERS, S5_GROUPS, S5_STATE, S5_GROUP), (2 * S5_GROUP) ** -0.5)
    s5_b_im = nrm(13, (N_S5_LAYERS, S5_GROUPS, S5_STATE, S5_GROUP), (2 * S5_GROUP) ** -0.5)
    s5_c_re = nrm(14, (N_S5_LAYERS, S5_GROUPS, S5_GROUP, S5_STATE), S5_STATE ** -0.25)
    s5_c_im = nrm(15, (N_S5_LAYERS, S5_GROUPS, S5_GROUP, S5_STATE), S5_STATE ** -0.25)
    s5_d = nrm(16, (N_S5_LAYERS, D_MODEL))
    s5_w_val = nrm(17, (N_S5_LAYERS, D_MODEL, D_MODEL), D_MODEL ** -0.5 * DN_BETA)
    s5_w_gate = nrm(18, (N_S5_LAYERS, D_MODEL, D_MODEL), D_MODEL ** -0.5)

    col_scale = jnp.ones((GDN_IN_DIM,), f32).at[2 * GDN_K_DIM:GDN_CONV_DIM].set(DN_BETA)
    gdn_w_in = nrm(19, (N_GDN_LAYERS, D_MODEL, GDN_IN_DIM), D_MODEL ** -0.5) * col_scale
    gdn_conv_w = nrm(20, (N_GDN_LAYERS, GDN_CONV_W, GDN_CONV_DIM), 0.5)
    gdn_a_log = jnp.log(jax.random.uniform(k[21], (N_GDN_LAYERS, GDN_V_HEADS), f32, minval=1.0, maxval=16.0))
    dt = jnp.exp(jax.random.uniform(k[22], (N_GDN_LAYERS, GDN_V_HEADS), f32,
                                    minval=math.log(1e-3), maxval=math.log(1e-1)))
    gdn_dt_bias = dt + jnp.log(-jnp.expm1(-dt))
    gdn_norm_w = 1.0 + nrm(23, (N_GDN_LAYERS, GDN_HEAD_V), 0.02)
    gdn_w_out = nrm(24, (N_GDN_LAYERS, GDN_V_DIM, D_MODEL), GDN_V_DIM ** -0.5 * DN_BETA)

    moe_router_w = nrm(25, (DEPTH, D_MODEL, N_EXPERTS), D_MODEL ** -0.5)
    moe_router_b = nrm(26, (DEPTH, N_EXPERTS), 0.01)
    moe_w_gu = nrm(27, (DEPTH, N_EXPERTS, D_MODEL, 2 * D_EXPERT), D_MODEL ** -0.5)
    moe_b_gu = nrm(28, (DEPTH, N_EXPERTS, 2 * D_EXPERT), 0.01)
    moe_w_down = nrm(29, (DEPTH, N_EXPERTS, D_EXPERT, D_MODEL), D_EXPERT ** -0.5 * DN_BETA)
    moe_b_down = nrm(30, (DEPTH, N_EXPERTS, D_MODEL), 0.01)

    return {
        "x_prompt": x_prompt, "x_sample": x_sample,
        "state_s5_re": state_s5_re, "state_s5_im": state_s5_im,
        "state_gdn": state_gdn, "state_gdn_conv": state_gdn_conv,
        "ln_mix_g": ln_mix_g, "ln_mix_b": ln_mix_b, "ln_ffn_g": ln_ffn_g, "ln_ffn_b": ln_ffn_b,
        "s5_a_re": s5_a_re, "s5_a_im": s5_a_im, "s5_log_step": s5_log_step,
        "s5_b_re": s5_b_re, "s5_b_im": s5_b_im, "s5_c_re": s5_c_re, "s5_c_im": s5_c_im,
        "s5_d": s5_d, "s5_w_val": s5_w_val, "s5_w_gate": s5_w_gate,
        "gdn_w_in": gdn_w_in, "gdn_conv_w": gdn_conv_w, "gdn_a_log": gdn_a_log,
        "gdn_dt_bias": gdn_dt_bias, "gdn_norm_w": gdn_norm_w, "gdn_w_out": gdn_w_out,
        "moe_router_w": moe_router_w, "moe_router_b": moe_router_b,
        "moe_w_gu": moe_w_gu, "moe_b_gu": moe_b_gu, "moe_w_down": moe_w_down, "moe_b_down": moe_b_down,
    }


def reference(x_prompt, x_sample, state_s5_re, state_s5_im, state_gdn, state_gdn_conv,
              ln_mix_g, ln_mix_b, ln_ffn_g, ln_ffn_b,
              s5_a_re, s5_a_im, s5_log_step, s5_b_re, s5_b_im, s5_c_re, s5_c_im, s5_d, s5_w_val, s5_w_gate,
              gdn_w_in, gdn_conv_w, gdn_a_log, gdn_dt_bias, gdn_norm_w, gdn_w_out,
              moe_router_w, moe_router_b, moe_w_gu, moe_b_gu, moe_w_down, moe_b_down):
    xp, xs = x_prompt, x_sample
    nbp = xp.shape[0]
    n_p = xp.shape[0] * xp.shape[1]
    s5_re_p, s5_im_p, s5_re_s, s5_im_s = [], [], [], []
    gdn_p, conv_p, gdn_s, conv_s = [], [], [], []
    for i in range(DEPTH):
        j = i // N_MIXERS
        if i % N_MIXERS == 0:
            prm = (s5_a_re[j], s5_a_im[j], s5_log_step[j], s5_b_re[j], s5_b_im[j],
                   s5_c_re[j], s5_c_im[j], s5_d[j], s5_w_val[j], s5_w_gate[j])
            h_zero = jnp.zeros((nbp, S5_GROUPS, S5_STATE), jnp.float32)
            mp, hr, hi = s5_mixer(xp, h_zero, h_zero, *prm)
            s5_re_p.append(hr)
            s5_im_p.append(hi)
            ms, hr, hi = s5_mixer(xs, state_s5_re[j], state_s5_im[j], *prm)
            s5_re_s.append(hr)
            s5_im_s.append(hi)
        else:
            prm = (gdn_w_in[j], gdn_conv_w[j], gdn_a_log[j], gdn_dt_bias[j], gdn_norm_w[j], gdn_w_out[j])
            s_zero = jnp.zeros((nbp, GDN_V_HEADS, GDN_HEAD_K, GDN_HEAD_V), jnp.float32)
            c_zero = jnp.zeros((nbp, GDN_CONV_W - 1, GDN_CONV_DIM), xp.dtype)
            mp, sp, cp = gdn_mixer(xp, s_zero, c_zero, *prm)
            gdn_p.append(sp)
            conv_p.append(cp)
            ms, ss, cs = gdn_mixer(xs, state_gdn[j], state_gdn_conv[j], *prm)
            gdn_s.append(ss)
            conv_s.append(cs)
        xp = layer_norm(DN_ALPHA * xp + mp, ln_mix_g[i], ln_mix_b[i])
        xs = layer_norm(DN_ALPHA * xs + ms, ln_mix_g[i], ln_mix_b[i])
        x_all = jnp.concatenate([xp.reshape(-1, D_MODEL), xs.reshape(-1, D_MODEL)], axis=0)
        f_all = moe_ffn(x_all, moe_router_w[i], moe_router_b[i], moe_w_gu[i], moe_b_gu[i],
                        moe_w_down[i], moe_b_down[i])
        x_all = layer_norm(DN_ALPHA * x_all + f_all, ln_ffn_g[i], ln_ffn_b[i])
        xp = x_all[:n_p].reshape(xp.shape)
        xs = x_all[n_p:].reshape(xs.shape)
    return (xp, xs,
            jnp.stack(s5_re_p), jnp.stack(s5_im_p), jnp.stack(gdn_p), jnp.stack(conv_p),
            jnp.stack(s5_re_s), jnp.stack(s5_im_s), jnp.stack(gdn_s), jnp.stack(conv_s))
```

```python
import functools
import math

import jax
import jax.numpy as jnp
from jax import lax
from jax.experimental import pallas as pl
from jax.experimental.pallas import tpu as pltpu

F32 = jnp.float32
BF16 = jnp.bfloat16
HIGHEST = lax.Precision.HIGHEST

TOP_K = 4
SWIGLU_ALPHA = 1.702
SWIGLU_LIMIT = 7.0
LN_EPS = 1e-5
RMS_EPS = 1e-6
L2_EPS = 1e-6

LANES = 128
SUBLANES = 8
VMEM_CAP = 56 << 20

S5_GROUP_BLOCK = 16
GDN_CHUNK = 64
MOE_ROW_BLOCK = 256


def _tile(dim, target, align):
    t = min(target, dim) // align * align
    while t >= align:
        if dim % t == 0:
            return t
        t -= align
    return dim


def _cparams(n_axes, vmem_bytes):
    return pltpu.CompilerParams(
        dimension_semantics=("arbitrary",) * n_axes,
        vmem_limit_bytes=int(min(max(vmem_bytes, 16 << 20), VMEM_CAP)))


def _sigmoid(x):
    return 1.0 / (1.0 + jnp.exp(-x))


def _silu(x):
    return x * _sigmoid(x)


def _layer_norm(y, g, b):
    mu = jnp.mean(y, axis=-1, keepdims=True)
    yc = y - mu
    var = jnp.mean(yc * yc, axis=-1, keepdims=True)
    return yc * lax.rsqrt(var + LN_EPS) * g + b


def _mm_body(x_ref, *refs, n_w, epilogue):
    w_refs = refs[:n_w]
    o_ref = refs[n_w]
    wb_refs = refs[n_w + 1:]

    @pl.when(pl.program_id(1) == 0)
    def _():
        for w, wb in zip(w_refs, wb_refs):
            wb[...] = w[...].astype(BF16)

    x = x_ref[...]
    accs = [jnp.dot(x, wb[...], preferred_element_type=F32) for wb in wb_refs]
    o_ref[...] = epilogue(*accs).astype(o_ref.dtype)


def _matmul(x, ws, layer, col0, n_cols, epilogue, out_dtype, tn_target=512, tm_target=1024, name="mm"):
    m, k = x.shape
    tn = _tile(n_cols, tn_target, LANES)
    tm = _tile(m, tm_target, SUBLANES)
    assert col0 % tn == 0
    c0 = col0 // tn
    n_w = len(ws)
    w_spec = pl.BlockSpec((None, k, tn), lambda j, i: (layer, 0, c0 + j))
    vmem = 2 * tm * k * 2 + n_w * (2 * k * tn * 4 + k * tn * 2) + 2 * tm * tn * 4 + (n_w + 1) * tm * tn * 4
    return pl.pallas_call(
        functools.partial(_mm_body, n_w=n_w, epilogue=epilogue),
        grid=(n_cols // tn, m // tm),
        in_specs=[pl.BlockSpec((tm, k), lambda j, i: (i, 0))] + [w_spec] * n_w,
        out_specs=pl.BlockSpec((tm, tn), lambda j, i: (i, j)),
        out_shape=jax.ShapeDtypeStruct((m, n_cols), out_dtype),
        scratch_shapes=[pltpu.VMEM((k, tn), BF16)] * n_w,
        compiler_params=_cparams(2, vmem + (4 << 20)),
        name=name,
    )(x, *ws)


def _gated_epilogue(val, gate):
    return val * _sigmoid(gate)


def _identity_epilogue(acc):
    return acc


def _ln_router_body(x_ref, m_ref, g_ref, b_ref, rw_ref, rb_ref, xo_ref, lg_ref, *, alpha):
    xn = _layer_norm(alpha * x_ref[...] + m_ref[...], g_ref[...], b_ref[...])
    xo_ref[...] = xn
    lg_ref[...] = jnp.dot(xn, rw_ref[...], preferred_element_type=F32, precision=HIGHEST) + rb_ref[...]


def _ln_router(x, m, g, b, rw, rb, alpha):
    t, d = x.shape
    tm = _tile(t, 256, SUBLANES)
    ne = rw.shape[1]
    row = pl.BlockSpec((tm, d), lambda i: (i, 0))
    vec = pl.BlockSpec((1, d), lambda i: (0, 0))
    return pl.pallas_call(
        functools.partial(_ln_router_body, alpha=alpha),
        grid=(t // tm,),
        in_specs=[row, row, vec, vec, pl.BlockSpec((d, ne), lambda i: (0, 0)), pl.BlockSpec((1, ne), lambda i: (0, 0))],
        out_specs=[row, pl.BlockSpec((tm, ne), lambda i: (i, 0))],
        out_shape=[jax.ShapeDtypeStruct((t, d), F32), jax.ShapeDtypeStruct((t, ne), F32)],
        compiler_params=_cparams(1, 10 * tm * d * 4 + 4 * d * ne * 4),
        name="ln_router",
    )(x, m, g, b, rw, rb)


def _s5_body(x_ref, wb_ref, wc_ref, tab_ref, d_ref, h0r_ref, h0i_ref,
             z_ref, hlr_ref, hli_ref, hre, him, car, *, rows, seq_len, nc, time_sliced):
    u = x_ref[...]
    bu = jnp.dot(u.astype(BF16), wb_ref[...], preferred_element_type=F32)
    hre[...] = bu[:, :nc]
    him[...] = bu[:, nc:]

    def scan8(i, carry):
        sl = pl.ds(pl.multiple_of(i * SUBLANES, SUBLANES), SUBLANES)
        xr = hre[sl, :]
        xi = him[sl, :]
        for n, d in enumerate((1, 2, 4)):
            ar = tab_ref[2 * n]
            ai = tab_ref[2 * n + 1]
            sr = pltpu.roll(xr, d, 0)
            si = pltpu.roll(xi, d, 0)
            xr, xi = xr + (ar * sr - ai * si), xi + (ar * si + ai * sr)
        pr = tab_ref[6]
        pi_ = tab_ref[7]
        cr, ci = carry
        xr = xr + (pr * cr - pi_ * ci)
        xi = xi + (pr * ci + pi_ * cr)
        hre[sl, :] = xr
        him[sl, :] = xi
        return xr[SUBLANES - 1:SUBLANES, :], xi[SUBLANES - 1:SUBLANES, :]

    if time_sliced:
        b = pl.program_id(1)
        t = pl.program_id(2)

        @pl.when(t == 0)
        def _():
            car[0:1, :] = h0r_ref[pl.ds(b, 1), :]
            car[1:2, :] = h0i_ref[pl.ds(b, 1), :]

        cr, ci = lax.fori_loop(0, rows // SUBLANES, scan8, (car[0:1, :], car[1:2, :]))
        car[0:1, :] = cr
        car[1:2, :] = ci

        @pl.when(t == pl.num_programs(2) - 1)
        def _():
            hlr_ref[pl.ds(b, 1), :] = cr
            hli_ref[pl.ds(b, 1), :] = ci
    else:
        per_seq = seq_len // SUBLANES

        def one_seq(s, _):
            carry = (h0r_ref[pl.ds(s, 1), :], h0i_ref[pl.ds(s, 1), :])
            cr, ci = lax.fori_loop(s * per_seq, (s + 1) * per_seq, scan8, carry)
            hlr_ref[pl.ds(s, 1), :] = cr
            hli_ref[pl.ds(s, 1), :] = ci
            return 0

        lax.fori_loop(0, rows // seq_len, one_seq, 0)

    hcat = jnp.concatenate([hre[...].astype(BF16), him[...].astype(BF16)], axis=1)
    y = jnp.dot(hcat, wc_ref[...], preferred_element_type=F32) + d_ref[...] * u
    z = 0.5 * y * (1.0 + lax.erf(y * (1.0 / math.sqrt(2.0))))
    z_ref[...] = z.astype(z_ref.dtype)


def _s5_prepare(a_re, a_im, log_step, b_re, b_im, c_re, c_im):
    g, p = a_re.shape
    cg = b_re.shape[-1]
    gb = min(S5_GROUP_BLOCK, g)
    nblk = g // gb
    step = jnp.exp(log_step)
    zr, zi = a_re * step, a_im * step

    def power(k):
        mag = jnp.exp(k * zr)
        return mag * jnp.cos(k * zi), mag * jnp.sin(k * zi)

    lbr, lbi = power(1.0)
    den = a_re * a_re + a_im * a_im
    nr, ni = lbr - 1.0, lbi
    cf_r = (nr * a_re + ni * a_im) / den
    cf_i = (ni * a_re - nr * a_im) / den
    bb_r = cf_r[..., None] * b_re - cf_i[..., None] * b_im
    bb_i = cf_r[..., None] * b_im + cf_i[..., None] * b_re
    eye = jnp.eye(gb, dtype=F32)

    def blockdiag_in(m):
        m = m.reshape(nblk, gb, p, cg)
        return jnp.einsum('agpc,gh->agchp', m, eye).reshape(nblk, gb * cg, gb * p)

    def blockdiag_out(m):
        m = m.reshape(nblk, gb, cg, p)
        return jnp.einsum('agcp,gh->agphc', m, eye).reshape(nblk, gb * p, gb * cg)

    wb = jnp.concatenate([blockdiag_in(bb_r), blockdiag_in(bb_i)], axis=2).astype(BF16)
    wc = jnp.concatenate([blockdiag_out(c_re), blockdiag_out(-c_im)], axis=1).astype(BF16)

    rows = jnp.arange(SUBLANES, dtype=F32)[:, None]
    tabs = []
    for d in (1, 2, 4):
        pr, pi_ = power(float(d))
        keep = (rows >= d).astype(F32)
        tabs += [keep * pr.reshape(nblk, 1, gb * p), keep * pi_.reshape(nblk, 1, gb * p)]
    kk = (rows + 1.0)[None]
    mag = jnp.exp(kk * zr.reshape(nblk, 1, gb * p))
    ang = kk * zi.reshape(nblk, 1, gb * p)
    tabs += [mag * jnp.cos(ang), mag * jnp.sin(ang)]
    tab = jnp.stack(tabs, axis=1)
    return wb, wc, tab


def _s5_scan(x_all, row0, n_seq, seq_len, prm, d_skip, h0r, h0i, z_prev):
    wb, wc, tab = prm
    t_all, d = x_all.shape
    nblk, kin, n2 = wb.shape
    nc = n2 // 2
    time_sliced = seq_len >= 512
    if time_sliced:
        rows = _tile(seq_len, 512, SUBLANES)
        n_t = seq_len // rows
        grid = (nblk, n_seq, n_t)
        nb_blk = n_seq
        row_map = lambda g, b, t: (row0 // rows + b * n_t + t, g)
        st_map = lambda g, b, t: (0, g)
    else:
        seqs = _tile(n_seq, max(1024 // seq_len, 1), SUBLANES)
        rows = seqs * seq_len
        grid = (nblk, n_seq // seqs, 1)
        nb_blk = seqs
        row_map = lambda g, b, t: (row0 // rows + b, g)
        st_map = lambda g, b, t: (b, g)
    assert row0 % rows == 0
    x_spec = pl.BlockSpec((rows, kin), row_map)
    st_spec = pl.BlockSpec((nb_blk, nc), st_map)
    body = functools.partial(_s5_body, rows=rows, seq_len=seq_len, nc=nc, time_sliced=time_sliced)
    in_specs = [x_spec,
                pl.BlockSpec((None, kin, n2), lambda g, b, t: (g, 0, 0)),
                pl.BlockSpec((None, n2, kin), lambda g, b, t: (g, 0, 0)),
                pl.BlockSpec((None, 8, SUBLANES, nc), lambda g, b, t: (g, 0, 0, 0)),
                pl.BlockSpec((1, kin), lambda g, b, t: (0, g)),
                st_spec, st_spec]
    args = [x_all, wb, wc, tab, d_skip.reshape(1, d), h0r, h0i]
    n_in = len(args)
    in_specs.append(pl.BlockSpec(memory_space=pl.ANY))
    args.append(z_prev)
    return pl.pallas_call(
        lambda *refs: body(*refs[:n_in], *refs[n_in + 1:]),
        grid=grid,
        in_specs=in_specs,
        out_specs=[x_spec, st_spec, st_spec],
        out_shape=[jax.ShapeDtypeStruct((t_all, d), BF16),
                   jax.ShapeDtypeStruct(h0r.shape, F32), jax.ShapeDtypeStruct(h0i.shape, F32)],
        scratch_shapes=[pltpu.VMEM((rows, nc), F32), pltpu.VMEM((rows, nc), F32), pltpu.VMEM((SUBLANES, nc), F32)],
        input_output_aliases={n_in: 0},
        compiler_params=_cparams(3, 8 * rows * n2 * 4 + 8 * kin * n2 * 2),
        name="s5_scan",
    )(*args)


def _unit_lower_inverse(low, eye):
    c = low.shape[0]
    a = -low
    t = eye + a
    k = 1
    while 2 * k < c:
        a = jnp.dot(a, a, preferred_element_type=F32, precision=HIGHEST)
        t = t + jnp.dot(t, a, preferred_element_type=F32, precision=HIGHEST)
        k *= 2
    return t


def _gdn_body(q_ref, k_ref, v_ref, z_ref, ba_ref, cwq_ref, cwk_ref, cwv_ref, c0q_ref, c0k_ref, c0v_ref,
              acoef_ref, dtb_ref, nw_ref, s0_ref, o_ref, sn_ref,
              xq, xk, xv, qs, ks, vs, bs, gs, *, rows, chunk, n_vh, dk, dv):
    hp = pl.program_id(1)
    t = pl.program_id(2)
    halo = SUBLANES - 3

    @pl.when(t == 0)
    def _():
        xq[halo:SUBLANES, :] = c0q_ref[...]
        xk[halo:SUBLANES, :] = c0k_ref[...]
        xv[halo:SUBLANES, :] = c0v_ref[...]
        sn_ref[...] = s0_ref[...]

    def conv(x_scr, blk_ref, cw_ref):
        x_scr[SUBLANES:SUBLANES + rows, :] = blk_ref[...]
        y = cw_ref[0:1, :] * x_scr[halo:halo + rows, :]
        for i in range(1, 4):
            y = y + cw_ref[i:i + 1, :] * x_scr[halo + i:halo + i + rows, :]
        last = x_scr[rows + halo:rows + SUBLANES, :]
        x_scr[halo:SUBLANES, :] = last
        return _silu(y)

    def l2n(x):
        return x * lax.rsqrt(jnp.sum(x * x, axis=-1, keepdims=True) + L2_EPS)

    qs[...] = l2n(conv(xq, q_ref, cwq_ref)) * (dk ** -0.5)
    ks[...] = l2n(conv(xk, k_ref, cwk_ref))
    vs[...] = conv(xv, v_ref, cwv_ref)
    ba = ba_ref[...]
    bs[...] = _sigmoid(ba)
    sp = ba + dtb_ref[...]
    gs[...] = acoef_ref[...] * (jnp.maximum(sp, 0.0) + jnp.log1p(jnp.exp(-jnp.abs(sp))))

    c = chunk
    ri = lax.broadcasted_iota(jnp.int32, (c, c), 0)
    ci = lax.broadcasted_iota(jnp.int32, (c, c), 1)
    causal = ri >= ci
    strict = ri > ci
    eye = (ri == ci).astype(F32)
    tri = causal.astype(F32)
    lane = lax.broadcasted_iota(jnp.int32, (c, LANES), 1)
    sel_rows = lax.broadcasted_iota(jnp.int32, (SUBLANES, LANES), 0)
    sel_lane = lax.broadcasted_iota(jnp.int32, (SUBLANES, LANES), 1)

    def chunk_step(ic, _):
        sl = pl.ds(pl.multiple_of(ic * c, c), c)
        qc = qs[sl, :]
        kc = ks[sl, :]
        gcs = jnp.dot(tri, gs[sl, :], preferred_element_type=F32, precision=HIGHEST)
        bc_all = bs[sl, :]
        kq = lax.dot_general(jnp.concatenate([qc, kc], axis=0).astype(BF16), kc.astype(BF16),
                             (((1,), (1,)), ((), ())), preferred_element_type=F32)
        qk = kq[:c, :]
        kk = kq[c:, :]
        for hh in range(2):
            lane_b = 2 * hp + hh
            lane_g = n_vh + lane_b
            beta = jnp.sum(jnp.where(lane == lane_b, bc_all, 0.0), axis=1, keepdims=True)
            gcol = jnp.sum(jnp.where(lane == lane_g, gcs, 0.0), axis=1, keepdims=True)
            pick = ((sel_rows == 0) & (sel_lane == lane_g)).astype(F32)
            grow = lax.dot_general(pick, gcs, (((1,), (1,)), ((), ())),
                                   preferred_element_type=F32, precision=HIGHEST)[0:1, :]
            decay = jnp.exp(jnp.where(causal, gcol - grow, -jnp.inf))
            low = jnp.where(strict, kk * beta * decay, 0.0)
            tmat = _unit_lower_inverse(low, eye)
            eg = jnp.exp(gcol)
            vb = vs[sl, hh * dv:(hh + 1) * dv] * beta
            kbg = kc * (beta * eg)
            uw = jnp.dot(tmat.astype(BF16), jnp.concatenate([vb, kbg], axis=1).astype(BF16),
                         preferred_element_type=F32)
            u = uw[:, :dv]
            w = uw[:, dv:]
            s = sn_ref[hh]
            sb = s.astype(BF16)
            wq = jnp.dot(jnp.concatenate([w, qc * eg], axis=0).astype(BF16), sb, preferred_element_type=F32)
            v_new = u - wq[:c, :]
            vnb = v_new.astype(BF16)
            o = wq[c:, :] + jnp.dot((qk * decay).astype(BF16), vnb, preferred_element_type=F32)
            g_last = gcol[c - 1:c, :]
            kdec = kc * jnp.exp(g_last - gcol)
            sn_ref[hh] = s * jnp.exp(g_last) + lax.dot_general(
                kdec.astype(BF16), vnb, (((0,), (0,)), ((), ())), preferred_element_type=F32)
            o = o * lax.rsqrt(jnp.mean(o * o, axis=-1, keepdims=True) + RMS_EPS) * nw_ref[...]
            o = o * _silu(z_ref[sl, hh * dv:(hh + 1) * dv])
            o_ref[sl, hh * dv:(hh + 1) * dv] = o.astype(o_ref.dtype)
        return 0

    lax.fori_loop(0, rows // c, chunk_step, 0)


def _gdn_core(proj, ba, row0, n_seq, seq_len, conv_w, conv0, conv_idx, acoef, dtb, norm_w, s0, s0_idx, o_prev):
    t_all = proj.shape[0]
    n_vh, dk, dv = s0.shape[2], s0.shape[3], s0.shape[4]
    n_kh = n_vh // 2
    k_dim = n_kh * dk
    v_dim = n_vh * dv
    if seq_len >= GDN_CHUNK:
        rows = _tile(seq_len, 256, GDN_CHUNK)
        chunk = GDN_CHUNK
    else:
        rows = seq_len
        chunk = seq_len
    n_t = seq_len // rows
    assert row0 % rows == 0 and dk == LANES and dv == LANES
    r0 = row0 // rows
    kb = k_dim // dk
    rmap = lambda off: (lambda b, h, t: (r0 + b * n_t + t, off + h))
    cmap = lambda off: (lambda b, h, t: (0, off + h))
    c0map = lambda off: (lambda b, h, t: (conv_idx, b, 0, off + h))
    vb_off = 2 * k_dim // (2 * dv)
    zb_off = (2 * k_dim + v_dim) // (2 * dv)
    vec = pl.BlockSpec((1, LANES), lambda b, h, t: (0, 0))
    in_specs = [
        pl.BlockSpec((rows, dk), rmap(0)),
        pl.BlockSpec((rows, dk), rmap(kb)),
        pl.BlockSpec((rows, 2 * dv), rmap(vb_off)),
        pl.BlockSpec((rows, 2 * dv), rmap(zb_off)),
        pl.BlockSpec((rows, LANES), lambda b, h, t: (r0 + b * n_t + t, 0)),
        pl.BlockSpec((4, dk), cmap(0)),
        pl.BlockSpec((4, dk), cmap(kb)),
        pl.BlockSpec((4, 2 * dv), cmap(vb_off)),
        pl.BlockSpec((None, None, 3, dk), c0map(0)),
        pl.BlockSpec((None, None, 3, dk), c0map(kb)),
        pl.BlockSpec((None, None, 3, 2 * dv), c0map(vb_off)),
        vec, vec, vec,
        pl.BlockSpec((None, None, 2, dk, dv), lambda b, h, t: (s0_idx, b, h, 0, 0)),
    ]
    args = [proj, proj, proj, proj, ba, conv_w, conv_w, conv_w, conv0, conv0, conv0, acoef, dtb, norm_w, s0]
    o_spec = pl.BlockSpec((rows, 2 * dv), lambda b, h, t: (r0 + b * n_t + t, h))
    body = functools.partial(_gdn_body, rows=rows, chunk=chunk, n_vh=n_vh, dk=dk, dv=dv)
    n_in = len(args)
    in_specs.append(pl.BlockSpec(memory_space=pl.ANY))
    args.append(o_prev)
    f32s = lambda w: pltpu.VMEM((rows, w), F32)
    return pl.pallas_call(
        lambda *refs: body(*refs[:n_in], *refs[n_in + 1:]),
        grid=(n_seq, n_kh, n_t),
        in_specs=in_specs,
        out_specs=[o_spec, pl.BlockSpec((None, 2, dk, dv), lambda b, h, t: (b, h, 0, 0))],
        out_shape=[jax.ShapeDtypeStruct((t_all, v_dim), BF16),
                   jax.ShapeDtypeStruct((n_seq, n_vh, dk, dv), F32)],
        scratch_shapes=[pltpu.VMEM((rows + SUBLANES, dk), F32), pltpu.VMEM((rows + SUBLANES, dk), F32),
                        pltpu.VMEM((rows + SUBLANES, 2 * dv), F32),
                        f32s(dk), f32s(dk), f32s(2 * dv), f32s(LANES), f32s(LANES)],
        input_output_aliases={n_in: 0},
        compiler_params=_cparams(3, 40 * rows * LANES * 4 + (8 << 20)),
        name="gdn_core",
    )(*args)


def _gather_body(tok_ref, nval_ref, x_hbm, o_ref, buf, sem, *, rb):
    i = pl.program_id(0)
    nv = nval_ref[i]

    @pl.when(nv < rb)
    def _():
        buf[...] = jnp.zeros_like(buf)

    def row_copy(r):
        tok = tok_ref[i * rb + r]
        return pltpu.make_async_copy(x_hbm.at[pl.ds(tok, 1), :], buf.at[pl.ds(r, 1), :], sem)

    def start(r, c):
        row_copy(r).start()
        return c

    def wait(r, c):
        row_copy(r).wait()
        return c

    lax.fori_loop(0, nv, start, 0)
    lax.fori_loop(0, nv, wait, 0)
    o_ref[...] = buf[...].astype(o_ref.dtype)


def _moe_gather(x, tok_buf, nvalid, rb):
    t, d = x.shape
    cap = tok_buf.shape[0]
    grid_spec = pltpu.PrefetchScalarGridSpec(
        num_scalar_prefetch=2,
        grid=(cap // rb,),
        in_specs=[pl.BlockSpec(memory_space=pl.ANY)],
        out_specs=pl.BlockSpec((rb, d), lambda i, tok, nv: (i, 0)),
        scratch_shapes=[pltpu.VMEM((rb, d), F32), pltpu.SemaphoreType.DMA(())],
    )
    return pl.pallas_call(
        functools.partial(_gather_body, rb=rb),
        grid_spec=grid_spec,
        out_shape=jax.ShapeDtypeStruct((cap, d), BF16),
        compiler_params=_cparams(1, 4 * rb * d * 4),
        name="moe_gather",
    )(tok_buf, nvalid, x)


def _moe_gu_body(be_ref, nu_ref, x_ref, wg_ref, wl_ref, bg_ref, bl_ref, h_ref, wgb, wlb):
    i = pl.program_id(1)
    prev = be_ref[jnp.maximum(i - 1, 0)]
    fresh = (i == 0) | (be_ref[i] != prev)

    @pl.when(fresh & (i < nu_ref[0]))
    def _():
        wgb[...] = wg_ref[...].astype(BF16)
        wlb[...] = wl_ref[...].astype(BF16)

    @pl.when(i < nu_ref[0])
    def _():
        x = x_ref[...]
        gate = jnp.dot(x, wgb[...], preferred_element_type=F32) + bg_ref[...]
        lin = jnp.dot(x, wlb[...], preferred_element_type=F32) + bl_ref[...]
        gate = jnp.minimum(gate, SWIGLU_LIMIT)
        lin = jnp.clip(lin, -SWIGLU_LIMIT, SWIGLU_LIMIT)
        h_ref[...] = (gate * _sigmoid(SWIGLU_ALPHA * gate) * (lin + 1.0)).astype(h_ref.dtype)

    @pl.when(i >= nu_ref[0])
    def _():
        h_ref[...] = jnp.zeros_like(h_ref)


def _moe_gu(xs, block_e, n_used, w_gu, b_gu, layer, rb):
    cap, d = xs.shape
    de = w_gu.shape[3] // 2
    tn = _tile(de, 256, LANES)
    nj = de // tn
    wmap = lambda off: (lambda j, i, be, nu: (layer, be[i], 0, off + j))
    bmap = lambda off: (lambda j, i, be, nu: (layer, be[i], 0, off + j))
    grid_spec = pltpu.PrefetchScalarGridSpec(
        num_scalar_prefetch=2,
        grid=(nj, cap // rb),
        in_specs=[pl.BlockSpec((rb, d), lambda j, i, be, nu: (i, 0)),
                  pl.BlockSpec((None, None, d, tn), wmap(0)),
                  pl.BlockSpec((None, None, d, tn), wmap(nj)),
                  pl.BlockSpec((None, None, 1, tn), bmap(0)),
                  pl.BlockSpec((None, None, 1, tn), bmap(nj))],
        out_specs=pl.BlockSpec((rb, tn), lambda j, i, be, nu: (i, j)),
        scratch_shapes=[pltpu.VMEM((d, tn), BF16), pltpu.VMEM((d, tn), BF16)],
    )
    b4 = b_gu.reshape(b_gu.shape[0], b_gu.shape[1], 1, b_gu.shape[2])
    return pl.pallas_call(
        _moe_gu_body,
        grid_spec=grid_spec,
        out_shape=jax.ShapeDtypeStruct((cap, de), BF16),
        compiler_params=_cparams(2, 2 * (2 * d * tn * 4) + 2 * d * tn * 2 + 4 * rb * d + 8 * rb * tn * 4 + (4 << 20)),
        name="moe_gu",
    )(block_e, n_used, xs, w_gu, w_gu, b4, b4)


def _moe_down_body(be_ref, nu_ref, h_ref, w_ref, b_ref, g_ref, y_ref, wb):
    i = pl.program_id(1)
    prev = be_ref[jnp.maximum(i - 1, 0)]
    fresh = (i == 0) | (be_ref[i] != prev)

    @pl.when(fresh & (i < nu_ref[0]))
    def _():
        wb[...] = w_ref[...].astype(BF16)

    @pl.when(i < nu_ref[0])
    def _():
        y = jnp.dot(h_ref[...], wb[...], preferred_element_type=F32) + b_ref[...]
        y_ref[...] = y * g_ref[...]

    @pl.when(i >= nu_ref[0])
    def _():
        y_ref[...] = jnp.zeros_like(y_ref)


def _moe_down(h, block_e, n_used, w_down, b_down, gate_buf, layer, rb):
    cap, de = h.shape
    d = w_down.shape[3]
    tn = _tile(d, 1024, LANES)
    grid_spec = pltpu.PrefetchScalarGridSpec(
        num_scalar_prefetch=2,
        grid=(d // tn, cap // rb),
        in_specs=[pl.BlockSpec((rb, de), lambda j, i, be, nu: (i, 0)),
                  pl.BlockSpec((None, None, de, tn), lambda j, i, be, nu: (layer, be[i], 0, j)),
                  pl.BlockSpec((None, None, 1, tn), lambda j, i, be, nu: (layer, be[i], 0, j)),
                  pl.BlockSpec((rb, 1), lambda j, i, be, nu: (i, 0))],
        out_specs=pl.BlockSpec((rb, tn), lambda j, i, be, nu: (i, j)),
        scratch_shapes=[pltpu.VMEM((de, tn), BF16)],
    )
    b4 = b_down.reshape(b_down.shape[0], b_down.shape[1], 1, b_down.shape[2])
    return pl.pallas_call(
        _moe_down_body,
        grid_spec=grid_spec,
        out_shape=jax.ShapeDtypeStruct((cap, d), F32),
        compiler_params=_cparams(2, 2 * de * tn * 4 + de * tn * 2 + 4 * rb * de + 6 * rb * tn * 4 + (4 << 20)),
        name="moe_down",
    )(block_e, n_used, h, w_down, b4, gate_buf.reshape(cap, 1))


def _combine_body(pos_ref, y_hbm, x_ref, g_ref, b_ref, xo_ref, xb_ref, buf, sem, *, tb, alpha):
    i = pl.program_id(0)

    def row_copy(n):
        p = pos_ref[i * (tb * TOP_K) + n]
        return pltpu.make_async_copy(y_hbm.at[pl.ds(p, 1), :], buf.at[n % TOP_K, pl.ds(n // TOP_K, 1), :], sem)

    def start(n, c):
        row_copy(n).start()
        return c

    def wait(n, c):
        row_copy(n).wait()
        return c

    lax.fori_loop(0, tb * TOP_K, start, 0)
    lax.fori_loop(0, tb * TOP_K, wait, 0)
    f = buf[0]
    for k in range(1, TOP_K):
        f = f + buf[k]
    xn = _layer_norm(alpha * x_ref[...] + f, g_ref[...], b_ref[...])
    xo_ref[...] = xn
    xb_ref[...] = xn.astype(BF16)


def _moe_combine_ln(y_sorted, pos, x, g, b, alpha):
    t, d = x.shape
    tb = _tile(t, 64, SUBLANES)
    row = lambda i, p: (i, 0)
    grid_spec = pltpu.PrefetchScalarGridSpec(
        num_scalar_prefetch=1,
        grid=(t // tb,),
        in_specs=[pl.BlockSpec(memory_space=pl.ANY),
                  pl.BlockSpec((tb, d), row),
                  pl.BlockSpec((1, d), lambda i, p: (0, 0)),
                  pl.BlockSpec((1, d), lambda i, p: (0, 0))],
        out_specs=[pl.BlockSpec((tb, d), row), pl.BlockSpec((tb, d), row)],
        scratch_shapes=[pltpu.VMEM((TOP_K, tb, d), F32), pltpu.SemaphoreType.DMA(())],
    )
    return pl.pallas_call(
        functools.partial(_combine_body, tb=tb, alpha=alpha),
        grid_spec=grid_spec,
        out_shape=[jax.ShapeDtypeStruct((t, d), F32), jax.ShapeDtypeStruct((t, d), BF16)],
        compiler_params=_cparams(1, (TOP_K + 8) * tb * d * 4),
        name="moe_combine_ln",
    )(pos, y_sorted, x, g, b)


def _route(logits, n_experts, rb):
    n_tok = logits.shape[0]
    top_val, top_idx = lax.top_k(logits[:, :n_experts], TOP_K)
    gates = jax.nn.softmax(top_val, axis=-1)
    n_assign = n_tok * TOP_K
    flat_e = top_idx.reshape(-1).astype(jnp.int32)
    order = jnp.argsort(flat_e).astype(jnp.int32)
    sorted_e = flat_e[order]
    counts = jnp.zeros((n_experts,), jnp.int32).at[flat_e].add(1)
    padded = (counts + rb - 1) // rb * rb
    starts = jnp.cumsum(counts) - counts
    pad_ends = jnp.cumsum(padded)
    pad_starts = pad_ends - padded
    dest = pad_starts[sorted_e] + jnp.arange(n_assign, dtype=jnp.int32) - starts[sorted_e]
    n_blocks = -(-n_assign // rb) + n_experts
    cap = n_blocks * rb
    tok_buf = jnp.zeros((cap,), jnp.int32).at[dest].set(order // TOP_K)
    gate_buf = jnp.zeros((cap,), F32).at[dest].set(gates.reshape(-1)[order])
    pos = jnp.zeros((n_assign,), jnp.int32).at[order].set(dest)
    blk0 = jnp.arange(n_blocks, dtype=jnp.int32) * rb
    block_e = jnp.minimum(jnp.searchsorted(pad_ends, blk0, side='right'), n_experts - 1).astype(jnp.int32)
    nvalid = jnp.clip(pad_starts[block_e] + counts[block_e] - blk0, 0, rb).astype(jnp.int32)
    nvalid = jnp.where(blk0 < pad_ends[-1], nvalid, 0)
    n_used = (pad_ends[-1] // rb).astype(jnp.int32).reshape(1)
    return tok_buf, gate_buf, pos, block_e, nvalid, n_used


def kernel(x_prompt, x_sample, state_s5_re, state_s5_im, state_gdn, state_gdn_conv, ln_mix_g, ln_mix_b, ln_ffn_g, ln_ffn_b, s5_a_re, s5_a_im, s5_log_step, s5_b_re, s5_b_im, s5_c_re, s5_c_im, s5_d, s5_w_val, s5_w_gate, gdn_w_in, gdn_conv_w, gdn_a_log, gdn_dt_bias, gdn_norm_w, gdn_w_out, moe_router_w, moe_router_b, moe_w_gu, moe_b_gu, moe_w_down, moe_b_down):
    nbp, sp, d = x_prompt.shape
    nbs, ss, _ = x_sample.shape
    depth = ln_mix_g.shape[0]
    n_p = nbp * sp
    n_s = nbs * ss
    n_experts = moe_router_w.shape[2]
    alpha = (2 * depth) ** 0.25
    gp = state_s5_re.shape[2] * state_s5_re.shape[3]
    n_vh, dk, dv = state_gdn.shape[2], state_gdn.shape[3], state_gdn.shape[4]
    conv_dim = state_gdn_conv.shape[3]
    v_dim = n_vh * dv
    rb = MOE_ROW_BLOCK

    x = jnp.concatenate([x_prompt.reshape(n_p, d), x_sample.reshape(n_s, d)], axis=0)
    xb = x.astype(BF16)
    s5_h0r = state_s5_re.reshape(state_s5_re.shape[0], nbs, gp)
    s5_h0i = state_s5_im.reshape(state_s5_im.shape[0], nbs, gp)
    rw_pad = jnp.pad(moe_router_w, ((0, 0), (0, 0), (0, LANES - n_experts)))
    rb_pad = jnp.pad(moe_router_b, ((0, 0), (0, LANES - n_experts))).reshape(depth, 1, LANES)

    s5_re_p, s5_im_p, s5_re_s, s5_im_s = [], [], [], []
    gdn_p, conv_p, gdn_s, conv_s = [], [], [], []
    for i in range(depth):
        j = i // 2
        if i % 2 == 0:
            prm = _s5_prepare(s5_a_re[j], s5_a_im[j], s5_log_step[j], s5_b_re[j], s5_b_im[j], s5_c_re[j], s5_c_im[j])
            zeros = jnp.zeros((nbp, gp), F32)
            z, hr, hi = _s5_scan(x, 0, nbp, sp, prm, s5_d[j], zeros, zeros, jnp.zeros((n_p + n_s, d), BF16))
            s5_re_p.append(hr)
            s5_im_p.append(hi)
            z, hr, hi = _s5_scan(x, n_p, nbs, ss, prm, s5_d[j], s5_h0r[j], s5_h0i[j], z)
            s5_re_s.append(hr)
            s5_im_s.append(hi)
            mix = _matmul(z, [s5_w_val, s5_w_gate], j, 0, d, _gated_epilogue, F32, tn_target=256, name="s5_out")
        else:
            proj = _matmul(xb, [gdn_w_in], j, 0, conv_dim + v_dim, _identity_epilogue, F32, name="gdn_in")
            ba = _matmul(xb, [gdn_w_in], j, conv_dim + v_dim, 2 * n_vh, _identity_epilogue, F32, name="gdn_in_ba")
            pad = jnp.zeros((n_vh,), F32)
            acoef = jnp.concatenate([pad, -jnp.exp(gdn_a_log[j])]).reshape(1, LANES)
            dtb = jnp.concatenate([pad, gdn_dt_bias[j]]).reshape(1, LANES)
            nw = gdn_norm_w[j].reshape(1, dv)
            c_zero = jnp.zeros((1, nbp, 3, conv_dim), F32)
            s_zero = jnp.zeros((1, nbp, n_vh, dk, dv), F32)
            o, s_new = _gdn_core(proj, ba, 0, nbp, sp, gdn_conv_w[j], c_zero, 0, acoef, dtb, nw, s_zero, 0,
                                 jnp.zeros((n_p + n_s, v_dim), BF16))
            gdn_p.append(s_new)
            o, s_new = _gdn_core(proj, ba, n_p, nbs, ss, gdn_conv_w[j], state_gdn_conv, j, acoef, dtb, nw,
                                 state_gdn, j, o)
            gdn_s.append(s_new)
            qkv_p = proj[:n_p, :conv_dim].reshape(nbp, sp, conv_dim)
            qkv_s = proj[n_p:, :conv_dim].reshape(nbs, ss, conv_dim)
            conv_p.append(jnp.concatenate([jnp.zeros((nbp, 3, conv_dim), F32), qkv_p[:, -3:]], axis=1)[:, -3:])
            conv_s.append(jnp.concatenate([state_gdn_conv[j], qkv_s[:, -3:]], axis=1)[:, -3:])
            mix = _matmul(o, [gdn_w_out], j, 0, d, _identity_epilogue, F32, tn_target=256, tm_target=512,
                          name="gdn_out")
        x, logits = _ln_router(x, mix, ln_mix_g[i].reshape(1, d), ln_mix_b[i].reshape(1, d),
                               rw_pad[i], rb_pad[i], alpha)
        tok_buf, gate_buf, pos, block_e, nvalid, n_used = _route(logits, n_experts, rb)
        xs = _moe_gather(x, tok_buf, nvalid, rb)
        h = _moe_gu(xs, block_e, n_used, moe_w_gu, moe_b_gu, i, rb)
        y = _moe_down(h, block_e, n_used, moe_w_down, moe_b_down, gate_buf, i, rb)
        x, xb = _moe_combine_ln(y, pos, x, ln_ffn_g[i].reshape(1, d), ln_ffn_b[i].reshape(1, d), alpha)

    gs = state_s5_re.shape[2:]
    st = lambda lst, nb: jnp.stack(lst).reshape((len(lst), nb) + gs)
    return (x[:n_p].reshape(nbp, sp, d), x[n_p:].reshape(nbs, ss, d),
            st(s5_re_p, nbp), st(s5_im_p, nbp), jnp.stack(gdn_p), jnp.stack(conv_p),
            st(s5_re_s, nbs), st(s5_im_s, nbs), jnp.stack(gdn_s), jnp.stack(conv_s))
```

```python
import functools
import math

import jax
import jax.numpy as jnp
from jax import lax
from jax.experimental import pallas as pl
from jax.experimental.pallas import tpu as pltpu

F32 = jnp.float32
BF16 = jnp.bfloat16
HIGHEST = lax.Precision.HIGHEST

TOP_K = 4
SWIGLU_ALPHA = 1.702
SWIGLU_LIMIT = 7.0
LN_EPS = 1e-5
RMS_EPS = 1e-6
L2_EPS = 1e-6

LANES = 128
SUBLANES = 8
VMEM_CAP = 56 << 20

S5_GROUP_BLOCK = 16
GDN_CHUNK = 64
GDN_INV_BLOCK = 16
MOE_ROW_BLOCK = 512
MOE_GATHER_BLOCK = 256


def _tile(dim, target, align):
    t = min(target, dim) // align * align
    while t >= align:
        if dim % t == 0:
            return t
        t -= align
    return dim


def _cparams(n_axes, vmem_bytes):
    return pltpu.CompilerParams(
        dimension_semantics=("arbitrary",) * n_axes,
        vmem_limit_bytes=int(min(max(vmem_bytes, 16 << 20), VMEM_CAP)))


def _sigmoid(x):
    return 1.0 / (1.0 + jnp.exp(-x))


def _silu(x):
    return x * _sigmoid(x)


def _layer_norm(y, g, b):
    mu = jnp.mean(y, axis=-1, keepdims=True)
    yc = y - mu
    var = jnp.mean(yc * yc, axis=-1, keepdims=True)
    return yc * lax.rsqrt(var + LN_EPS) * g + b


def _mm_body(x_ref, *refs, n_w, epilogue):
    w_refs = refs[:n_w]
    o_ref = refs[n_w]
    wb_refs = refs[n_w + 1:]

    @pl.when(pl.program_id(1) == 0)
    def _():
        for w, wb in zip(w_refs, wb_refs):
            wb[...] = w[...].astype(BF16)

    x = x_ref[...]
    accs = [jnp.dot(x, wb[...], preferred_element_type=F32) for wb in wb_refs]
    o_ref[...] = epilogue(*accs).astype(o_ref.dtype)


def _matmul(x, ws, layer, col0, n_cols, epilogue, out_dtype, tn_target=512, tm_target=1024, name="mm"):
    m, k = x.shape
    tn = _tile(n_cols, tn_target, LANES)
    tm = _tile(m, tm_target, SUBLANES)
    assert col0 % tn == 0
    c0 = col0 // tn
    n_w = len(ws)
    w_spec = pl.BlockSpec((None, k, tn), lambda j, i: (layer, 0, c0 + j))
    vmem = 2 * tm * k * 2 + n_w * (2 * k * tn * 4 + k * tn * 2) + 2 * tm * tn * 4 + (n_w + 1) * tm * tn * 4
    return pl.pallas_call(
        functools.partial(_mm_body, n_w=n_w, epilogue=epilogue),
        grid=(n_cols // tn, m // tm),
        in_specs=[pl.BlockSpec((tm, k), lambda j, i: (i, 0))] + [w_spec] * n_w,
        out_specs=pl.BlockSpec((tm, tn), lambda j, i: (i, j)),
        out_shape=jax.ShapeDtypeStruct((m, n_cols), out_dtype),
        scratch_shapes=[pltpu.VMEM((k, tn), BF16)] * n_w,
        compiler_params=_cparams(2, vmem + (4 << 20)),
        name=name,
    )(x, *ws)


def _gated_epilogue(val, gate):
    return val * _sigmoid(gate)


def _identity_epilogue(acc):
    return acc


def _ln_router_body(x_ref, m_ref, g_ref, b_ref, rw_ref, rb_ref, xo_ref, eo_ref, ro_ref, go_ref, cnt_ref,
                    *, alpha, n_experts):
    xn = _layer_norm(alpha * x_ref[...] + m_ref[...], g_ref[...], b_ref[...])
    xo_ref[...] = xn
    tm = xn.shape[0]
    lg = jnp.dot(xn, rw_ref[...], preferred_element_type=F32, precision=HIGHEST) + rb_ref[...]
    lane = lax.broadcasted_iota(jnp.int32, (tm, LANES), 1)
    lane_f = lane.astype(F32)
    lg = jnp.where(lane < n_experts, lg, -jnp.inf)

    @pl.when(pl.program_id(0) == 0)
    def _():
        cnt_ref[...] = jnp.zeros_like(cnt_ref)

    vals, hots = [], []
    e_out = jnp.zeros((tm, LANES), jnp.int32)
    for k in range(TOP_K):
        m = jnp.max(lg, axis=1, keepdims=True)
        idx = jnp.min(jnp.where(lg == m, lane_f, float(LANES)), axis=1, keepdims=True)
        hot = lane_f == idx
        vals.append(m)
        hots.append(hot)
        lg = jnp.where(hot, -jnp.inf, lg)
        e_out = jnp.where(lane == k, idx.astype(jnp.int32), e_out)
    eo_ref[...] = e_out

    ex = [jnp.exp(v - vals[0]) for v in vals]
    den = ex[0]
    for e in ex[1:]:
        den = den + e
    g_out = jnp.zeros((tm, LANES), F32)
    for k in range(TOP_K):
        g_out = jnp.where(lane == k, ex[k] / den, g_out)
    go_ref[...] = g_out

    hot_all = hots[0].astype(F32)
    for h in hots[1:]:
        hot_all = hot_all + h.astype(F32)
    ri = lax.broadcasted_iota(jnp.int32, (tm, tm), 0)
    ci = lax.broadcasted_iota(jnp.int32, (tm, tm), 1)
    before = jnp.dot((ri > ci).astype(BF16), hot_all.astype(BF16), preferred_element_type=F32) + cnt_ref[...]
    r_out = jnp.zeros((tm, LANES), jnp.int32)
    for k in range(TOP_K):
        rank = jnp.sum(jnp.where(hots[k], before, 0.0), axis=1, keepdims=True)
        r_out = jnp.where(lane == k, rank.astype(jnp.int32), r_out)
    ro_ref[...] = r_out
    cnt_ref[...] += jnp.sum(hot_all, axis=0, keepdims=True)


def _ln_router(x, m, g, b, rw, rb, alpha, n_experts):
    t, d = x.shape
    tm = _tile(t, 256, SUBLANES)
    row = pl.BlockSpec((tm, d), lambda i: (i, 0))
    vec = pl.BlockSpec((1, d), lambda i: (0, 0))
    small = pl.BlockSpec((tm, LANES), lambda i: (i, 0))
    one = pl.BlockSpec((1, LANES), lambda i: (0, 0))
    return pl.pallas_call(
        functools.partial(_ln_router_body, alpha=alpha, n_experts=n_experts),
        grid=(t // tm,),
        in_specs=[row, row, vec, vec, pl.BlockSpec((d, LANES), lambda i: (0, 0)), one],
        out_specs=[row, small, small, small, one],
        out_shape=[jax.ShapeDtypeStruct((t, d), F32), jax.ShapeDtypeStruct((t, LANES), jnp.int32),
                   jax.ShapeDtypeStruct((t, LANES), jnp.int32), jax.ShapeDtypeStruct((t, LANES), F32),
                   jax.ShapeDtypeStruct((1, LANES), F32)],
        compiler_params=_cparams(1, 10 * tm * d * 4 + 4 * d * LANES * 4),
        name="ln_router",
    )(x, m, g, b, rw, rb)


def _s5_body(x_ref, wb_ref, wc_ref, tab_ref, d_ref, h0r_ref, h0i_ref,
             z_ref, hlr_ref, hli_ref, hre, him, car, *, rows, seq_len, nc, time_sliced):
    u = x_ref[...]
    bu = jnp.dot(u.astype(BF16), wb_ref[...], preferred_element_type=F32)
    hre[...] = bu[:, :nc]
    him[...] = bu[:, nc:]

    def scan8(i, carry):
        sl = pl.ds(pl.multiple_of(i * SUBLANES, SUBLANES), SUBLANES)
        xr = hre[sl, :]
        xi = him[sl, :]
        for n, d in enumerate((1, 2, 4)):
            ar = tab_ref[2 * n]
            ai = tab_ref[2 * n + 1]
            sr = pltpu.roll(xr, d, 0)
            si = pltpu.roll(xi, d, 0)
            xr, xi = xr + (ar * sr - ai * si), xi + (ar * si + ai * sr)
        pr = tab_ref[6]
        pi_ = tab_ref[7]
        cr, ci = carry
        xr = xr + (pr * cr - pi_ * ci)
        xi = xi + (pr * ci + pi_ * cr)
        hre[sl, :] = xr
        him[sl, :] = xi
        return xr[SUBLANES - 1:SUBLANES, :], xi[SUBLANES - 1:SUBLANES, :]

    if time_sliced:
        b = pl.program_id(1)
        t = pl.program_id(2)

        @pl.when(t == 0)
        def _():
            car[0:1, :] = h0r_ref[pl.ds(b, 1), :]
            car[1:2, :] = h0i_ref[pl.ds(b, 1), :]

        cr, ci = lax.fori_loop(0, rows // SUBLANES, scan8, (car[0:1, :], car[1:2, :]))
        car[0:1, :] = cr
        car[1:2, :] = ci

        @pl.when(t == pl.num_programs(2) - 1)
        def _():
            hlr_ref[pl.ds(b, 1), :] = cr
            hli_ref[pl.ds(b, 1), :] = ci
    else:
        per_seq = seq_len // SUBLANES

        def one_seq(s, _):
            carry = (h0r_ref[pl.ds(s, 1), :], h0i_ref[pl.ds(s, 1), :])
            cr, ci = lax.fori_loop(s * per_seq, (s + 1) * per_seq, scan8, carry)
            hlr_ref[pl.ds(s, 1), :] = cr
            hli_ref[pl.ds(s, 1), :] = ci
            return 0

        lax.fori_loop(0, rows // seq_len, one_seq, 0)

    hcat = jnp.concatenate([hre[...].astype(BF16), him[...].astype(BF16)], axis=1)
    y = jnp.dot(hcat, wc_ref[...], preferred_element_type=F32) + d_ref[...] * u
    z = 0.5 * y * (1.0 + lax.erf(y * (1.0 / math.sqrt(2.0))))
    z_ref[...] = z.astype(z_ref.dtype)


def _s5_prepare(a_re, a_im, log_step, b_re, b_im, c_re, c_im):
    g, p = a_re.shape
    cg = b_re.shape[-1]
    gb = min(S5_GROUP_BLOCK, g)
    nblk = g // gb
    step = jnp.exp(log_step)
    zr, zi = a_re * step, a_im * step

    def power(k):
        mag = jnp.exp(k * zr)
        return mag * jnp.cos(k * zi), mag * jnp.sin(k * zi)

    lbr, lbi = power(1.0)
    den = a_re * a_re + a_im * a_im
    nr, ni = lbr - 1.0, lbi
    cf_r = (nr * a_re + ni * a_im) / den
    cf_i = (ni * a_re - nr * a_im) / den
    bb_r = cf_r[..., None] * b_re - cf_i[..., None] * b_im
    bb_i = cf_r[..., None] * b_im + cf_i[..., None] * b_re
    eye = jnp.eye(gb, dtype=F32)

    def blockdiag_in(m):
        m = m.reshape(nblk, gb, p, cg)
        return jnp.einsum('agpc,gh->agchp', m, eye).reshape(nblk, gb * cg, gb * p)

    def blockdiag_out(m):
        m = m.reshape(nblk, gb, cg, p)
        return jnp.einsum('agcp,gh->agphc', m, eye).reshape(nblk, gb * p, gb * cg)

    wb = jnp.concatenate([blockdiag_in(bb_r), blockdiag_in(bb_i)], axis=2).astype(BF16)
    wc = jnp.concatenate([blockdiag_out(c_re), blockdiag_out(-c_im)], axis=1).astype(BF16)

    rows = jnp.arange(SUBLANES, dtype=F32)[:, None]
    tabs = []
    for d in (1, 2, 4):
        pr, pi_ = power(float(d))
        keep = (rows >= d).astype(F32)
        tabs += [keep * pr.reshape(nblk, 1, gb * p), keep * pi_.reshape(nblk, 1, gb * p)]
    kk = (rows + 1.0)[None]
    mag = jnp.exp(kk * zr.reshape(nblk, 1, gb * p))
    ang = kk * zi.reshape(nblk, 1, gb * p)
    tabs += [mag * jnp.cos(ang), mag * jnp.sin(ang)]
    tab = jnp.stack(tabs, axis=1)
    return wb, wc, tab


def _s5_scan(x_all, row0, n_seq, seq_len, prm, d_skip, h0r, h0i, z_prev):
    wb, wc, tab = prm
    t_all, d = x_all.shape
    nblk, kin, n2 = wb.shape
    nc = n2 // 2
    time_sliced = seq_len >= 512
    if time_sliced:
        rows = _tile(seq_len, 512, SUBLANES)
        n_t = seq_len // rows
        grid = (nblk, n_seq, n_t)
        nb_blk = n_seq
        row_map = lambda g, b, t: (row0 // rows + b * n_t + t, g)
        st_map = lambda g, b, t: (0, g)
    else:
        seqs = _tile(n_seq, max(1024 // seq_len, 1), SUBLANES)
        rows = seqs * seq_len
        grid = (nblk, n_seq // seqs, 1)
        nb_blk = seqs
        row_map = lambda g, b, t: (row0 // rows + b, g)
        st_map = lambda g, b, t: (b, g)
    assert row0 % rows == 0
    x_spec = pl.BlockSpec((rows, kin), row_map)
    st_spec = pl.BlockSpec((nb_blk, nc), st_map)
    body = functools.partial(_s5_body, rows=rows, seq_len=seq_len, nc=nc, time_sliced=time_sliced)
    in_specs = [x_spec,
                pl.BlockSpec((None, kin, n2), lambda g, b, t: (g, 0, 0)),
                pl.BlockSpec((None, n2, kin), lambda g, b, t: (g, 0, 0)),
                pl.BlockSpec((None, 8, SUBLANES, nc), lambda g, b, t: (g, 0, 0, 0)),
                pl.BlockSpec((1, kin), lambda g, b, t: (0, g)),
                st_spec, st_spec]
    args = [x_all, wb, wc, tab, d_skip.reshape(1, d), h0r, h0i]
    n_in = len(args)
    in_specs.append(pl.BlockSpec(memory_space=pl.ANY))
    args.append(z_prev)
    return pl.pallas_call(
        lambda *refs: body(*refs[:n_in], *refs[n_in + 1:]),
        grid=grid,
        in_specs=in_specs,
        out_specs=[x_spec, st_spec, st_spec],
        out_shape=[jax.ShapeDtypeStruct((t_all, d), BF16),
                   jax.ShapeDtypeStruct(h0r.shape, F32), jax.ShapeDtypeStruct(h0i.shape, F32)],
        scratch_shapes=[pltpu.VMEM((rows, nc), F32), pltpu.VMEM((rows, nc), F32), pltpu.VMEM((SUBLANES, nc), F32)],
        input_output_aliases={n_in: 0},
        compiler_params=_cparams(3, 8 * rows * n2 * 4 + 8 * kin * n2 * 2),
        name="s5_scan",
    )(*args)


def _bdot(a, b):
    return jnp.dot(a.astype(BF16), b.astype(BF16), preferred_element_type=F32)


def _select_dot(sel, x, dims):
    hi = x.astype(BF16)
    r1 = x - hi.astype(F32)
    mid = r1.astype(BF16)
    lo = (r1 - mid.astype(F32)).astype(BF16)
    out = lax.dot_general(sel, hi, dims, preferred_element_type=F32)
    out = out + lax.dot_general(sel, mid, dims, preferred_element_type=F32)
    return out + lax.dot_general(sel, lo, dims, preferred_element_type=F32)


def _unit_lower_inverses(lows, ri, ci):
    c = lows[0].shape[0]
    blk = min(GDN_INV_BLOCK, c)
    eye = (ri == ci).astype(F32)
    diag = ri // blk == ci // blk
    a_s = [jnp.where(diag, -low, 0.0) for low in lows]
    t_s = [eye + a for a in a_s]
    k = 1
    while 2 * k < blk:
        a_s = [_bdot(a, a) for a in a_s]
        t_s = [t + _bdot(t, a) for t, a in zip(t_s, a_s)]
        k *= 2
    while blk < c:
        pair = (ri // (2 * blk) == ci // (2 * blk)) & (ri // blk != ci // blk)
        mids = [_bdot(jnp.where(pair, low, 0.0), t) for low, t in zip(lows, t_s)]
        t_s = [t - _bdot(t, m) for t, m in zip(t_s, mids)]
        blk *= 2
    return t_s


def _gdn_body(q_ref, k_ref, v_ref, z_ref, ba_ref, cwq_ref, cwk_ref, cwv_ref, c0q_ref, c0k_ref, c0v_ref,
              acoef_ref, dtb_ref, nw_ref, s0_ref, o_ref, sn_ref,
              xq, xk, xv, qs, ks, vs, *, rows, seg_rows, carried, n_vh, dk, dv):
    hp = pl.program_id(1)
    halo = SUBLANES - 3
    slot = SUBLANES + seg_rows
    n_seg = rows // seg_rows
    c = GDN_CHUNK
    sub = min(seg_rows, c)
    n_sub = c // sub

    def init():
        for s in range(n_seg):
            xq[s * slot + halo:s * slot + SUBLANES, :] = c0q_ref[s]
            xk[s * slot + halo:s * slot + SUBLANES, :] = c0k_ref[s]
            xv[s * slot + halo:s * slot + SUBLANES, :] = c0v_ref[s]
        sn_ref[...] = s0_ref[...]

    if carried:
        pl.when(pl.program_id(2) == 0)(init)
    else:
        init()

    def conv(x_scr, blk_ref, cw_ref, out_ref, post):
        for s in range(n_seg):
            base = s * slot
            x_scr[base + SUBLANES:base + slot, :] = blk_ref[s * seg_rows:(s + 1) * seg_rows, :]
            y = cw_ref[0:1, :] * x_scr[base + halo:base + halo + seg_rows, :]
            for i in range(1, 4):
                y = y + cw_ref[i:i + 1, :] * x_scr[base + halo + i:base + halo + i + seg_rows, :]
            if carried:
                x_scr[base + halo:base + SUBLANES, :] = x_scr[base + seg_rows + halo:base + slot, :]
            out_ref[s * seg_rows:(s + 1) * seg_rows, :] = post(_silu(y))

    def l2n(x):
        return x * lax.rsqrt(jnp.sum(x * x, axis=-1, keepdims=True) + L2_EPS)

    conv(xq, q_ref, cwq_ref, qs, lambda y: l2n(y) * (dk ** -0.5))
    conv(xk, k_ref, cwk_ref, ks, l2n)
    conv(xv, v_ref, cwv_ref, vs, lambda y: y)

    ba = ba_ref[...]
    beta_all = _sigmoid(ba)
    sp = ba + dtb_ref[...]
    g_all = acoef_ref[...] * (jnp.maximum(sp, 0.0) + jnp.log1p(jnp.exp(-jnp.abs(sp))))
    rr = lax.broadcasted_iota(jnp.int32, (rows, rows), 0)
    cc = lax.broadcasted_iota(jnp.int32, (rows, rows), 1)
    run = ((rr // sub == cc // sub) & (rr >= cc)).astype(BF16)
    mm_dims = (((1,), (0,)), ((), ()))
    gcs = _select_dot(run, g_all, mm_dims)

    ri = lax.broadcasted_iota(jnp.int32, (c, c), 0)
    ci = lax.broadcasted_iota(jnp.int32, (c, c), 1)
    same = ri // sub == ci // sub
    causal = same & (ri >= ci)
    strict = same & (ri > ci)
    lane = lax.broadcasted_iota(jnp.int32, (c, LANES), 1)
    sel_rows = lax.broadcasted_iota(jnp.int32, (SUBLANES, LANES), 0)
    sel_lane = lax.broadcasted_iota(jnp.int32, (SUBLANES, LANES), 1)
    nt_dims = (((1,), (1,)), ((), ()))
    tn_dims = (((0,), (0,)), ((), ()))
    n_chunks = rows // c

    chunks = range(n_chunks)
    probs = [(ic, hh) for ic in chunks for hh in range(2)]
    qc = [qs[ic * c:(ic + 1) * c, :] for ic in chunks]
    kc = [ks[ic * c:(ic + 1) * c, :] for ic in chunks]
    gcc = [gcs[ic * c:(ic + 1) * c, :] for ic in chunks]
    bcc = [beta_all[ic * c:(ic + 1) * c, :] for ic in chunks]
    kq = [lax.dot_general(jnp.concatenate([qc[ic], kc[ic]], axis=0).astype(BF16), kc[ic].astype(BF16), nt_dims,
                          preferred_element_type=F32) for ic in chunks]
    pick = ((sel_rows < 2) & (sel_lane == n_vh + 2 * hp + sel_rows)).astype(BF16)
    grows = [_select_dot(pick, gcc[ic], nt_dims) for ic in chunks]
    beta, gcol, decay, lows = {}, {}, {}, []
    for ic, hh in probs:
        beta[ic, hh] = jnp.sum(jnp.where(lane == 2 * hp + hh, bcc[ic], 0.0), axis=1, keepdims=True)
        gcol[ic, hh] = jnp.sum(jnp.where(lane == n_vh + 2 * hp + hh, gcc[ic], 0.0), axis=1, keepdims=True)
        decay[ic, hh] = jnp.exp(jnp.where(causal, gcol[ic, hh] - grows[ic][hh:hh + 1, :], -jnp.inf))
        lows.append(jnp.where(strict, kq[ic][c:, :] * beta[ic, hh] * decay[ic, hh], 0.0))
    tmats = _unit_lower_inverses(lows, ri, ci)
    u, w, qkm, qe = {}, {}, {}, {}
    for (ic, hh), tmat in zip(probs, tmats):
        eg = jnp.exp(gcol[ic, hh])
        vb = vs[ic * c:(ic + 1) * c, hh * dv:(hh + 1) * dv] * beta[ic, hh]
        uw = _bdot(tmat, jnp.concatenate([vb, kc[ic] * (beta[ic, hh] * eg)], axis=1))
        u[ic, hh] = uw[:, :dv]
        w[ic, hh] = uw[:, dv:]
        qkm[ic, hh] = kq[ic][:c, :] * decay[ic, hh]
        qe[ic, hh] = qc[ic] * eg

    state = {hh: sn_ref[0, hh] for hh in range(2)} if carried else None
    for ic in chunks:
        items = [(hh, g) for hh in range(2) for g in range(n_sub)]
        span = {g: (g * sub, (g + 1) * sub) for g in range(n_sub)}
        s_old = {(hh, g): state[hh] if carried else sn_ref[ic * n_sub + g, hh] for hh, g in items}
        wq = {(hh, g): _bdot(jnp.concatenate([w[ic, hh][span[g][0]:span[g][1], :],
                                              qe[ic, hh][span[g][0]:span[g][1], :]], axis=0), s_old[hh, g])
              for hh, g in items}
        vn = {(hh, g): u[ic, hh][span[g][0]:span[g][1], :] - wq[hh, g][:sub, :] for hh, g in items}
        for hh, g in items:
            a0, a1 = span[g]
            g_last = gcol[ic, hh][a1 - 1:a1, :]
            kdec = kc[ic][a0:a1, :] * jnp.exp(g_last - gcol[ic, hh][a0:a1, :])
            s_new = s_old[hh, g] * jnp.exp(g_last) + lax.dot_general(
                kdec.astype(BF16), vn[hh, g].astype(BF16), tn_dims, preferred_element_type=F32)
            if carried:
                state[hh] = s_new
            else:
                sn_ref[ic * n_sub + g, hh] = s_new
        for hh in range(2):
            v_new = vn[hh, 0] if n_sub == 1 else jnp.concatenate([vn[hh, g] for g in range(n_sub)], axis=0)
            q_s = wq[hh, 0][sub:, :] if n_sub == 1 else jnp.concatenate(
                [wq[hh, g][sub:, :] for g in range(n_sub)], axis=0)
            o = q_s + _bdot(qkm[ic, hh], v_new)
            o = o * lax.rsqrt(jnp.mean(o * o, axis=-1, keepdims=True) + RMS_EPS) * nw_ref[...]
            o = o * _silu(z_ref[ic * c:(ic + 1) * c, hh * dv:(hh + 1) * dv])
            o_ref[ic * c:(ic + 1) * c, hh * dv:(hh + 1) * dv] = o.astype(o_ref.dtype)
    if carried:
        for hh in range(2):
            sn_ref[0, hh] = state[hh]


def _gdn_core(proj, ba, row0, n_seq, seq_len, conv_w, conv0, conv_idx, acoef, dtb, norm_w, s0, s0_idx, o_prev):
    t_all = proj.shape[0]
    n_vh, dk, dv = s0.shape[2], s0.shape[3], s0.shape[4]
    n_kh = n_vh // 2
    k_dim = n_kh * dk
    v_dim = n_vh * dv
    carried = seq_len >= GDN_CHUNK
    if carried:
        seg_rows = rows = _tile(seq_len, 4 * GDN_CHUNK, GDN_CHUNK)
        sb = 1
    else:
        per_chunk = GDN_CHUNK // seq_len
        sb = _tile(n_seq, 2 * per_chunk, per_chunk)
        seg_rows = seq_len
        rows = sb * seq_len
    n_t = seq_len // seg_rows
    assert rows % GDN_CHUNK == 0 and row0 % rows == 0 and n_seq % sb == 0
    assert dk == LANES and dv == LANES and 2 * n_vh == LANES and seq_len >= 3
    r0 = row0 // rows
    kb = k_dim // dk
    vb_off = 2 * k_dim // (2 * dv)
    zb_off = (2 * k_dim + v_dim) // (2 * dv)
    rmap = lambda off: (lambda b, h, t: (r0 + b * n_t + t, off + h))
    cmap = lambda off: (lambda b, h, t: (0, off + h))
    c0map = lambda off: (lambda b, h, t: (conv_idx, b, 0, off + h))
    vec = pl.BlockSpec((1, LANES), lambda b, h, t: (0, 0))
    in_specs = [
        pl.BlockSpec((rows, dk), rmap(0)),
        pl.BlockSpec((rows, dk), rmap(kb)),
        pl.BlockSpec((rows, 2 * dv), rmap(vb_off)),
        pl.BlockSpec((rows, 2 * dv), rmap(zb_off)),
        pl.BlockSpec((rows, LANES), lambda b, h, t: (r0 + b * n_t + t, 0)),
        pl.BlockSpec((4, dk), cmap(0)),
        pl.BlockSpec((4, dk), cmap(kb)),
        pl.BlockSpec((4, 2 * dv), cmap(vb_off)),
        pl.BlockSpec((None, sb, 3, dk), c0map(0)),
        pl.BlockSpec((None, sb, 3, dk), c0map(kb)),
        pl.BlockSpec((None, sb, 3, 2 * dv), c0map(vb_off)),
        vec, vec, vec,
        pl.BlockSpec((None, sb, 2, dk, dv), lambda b, h, t: (s0_idx, b, h, 0, 0)),
    ]
    args = [proj, proj, proj, proj, ba, conv_w, conv_w, conv_w, conv0, conv0, conv0, acoef, dtb, norm_w, s0]
    o_spec = pl.BlockSpec((rows, 2 * dv), lambda b, h, t: (r0 + b * n_t + t, h))
    body = functools.partial(_gdn_body, rows=rows, seg_rows=seg_rows, carried=carried, n_vh=n_vh, dk=dk, dv=dv)
    n_in = len(args)
    in_specs.append(pl.BlockSpec(memory_space=pl.ANY))
    args.append(o_prev)
    xrows = (rows // seg_rows) * (SUBLANES + seg_rows)
    return pl.pallas_call(
        lambda *refs: body(*refs[:n_in], *refs[n_in + 1:]),
        grid=(n_seq // sb, n_kh, n_t),
        in_specs=in_specs,
        out_specs=[o_spec, pl.BlockSpec((sb, 2, dk, dv), lambda b, h, t: (b, h, 0, 0))],
        out_shape=[jax.ShapeDtypeStruct((t_all, v_dim), BF16),
                   jax.ShapeDtypeStruct((n_seq, n_vh, dk, dv), F32)],
        scratch_shapes=[pltpu.VMEM((xrows, dk), F32), pltpu.VMEM((xrows, dk), F32),
                        pltpu.VMEM((xrows, 2 * dv), F32),
                        pltpu.VMEM((rows, dk), F32), pltpu.VMEM((rows, dk), F32), pltpu.VMEM((rows, 2 * dv), F32)],
        input_output_aliases={n_in: 0},
        compiler_params=_cparams(3, 48 * rows * LANES * 4 + 8 * sb * 2 * dk * dv * 4 + (8 << 20)),
        name="gdn_core",
    )(*args)


def _double_buffered(i, n, issue, wait):
    slot = i % 2

    @pl.when(i == 0)
    def _():
        issue(i, slot)

    @pl.when(i + 1 < n)
    def _():
        issue(i + 1, 1 - slot)

    wait(i, slot)
    return slot


def _gather_body(tok_ref, nval_ref, x_hbm, o_ref, buf, sem, *, rb):
    i = pl.program_id(0)

    def row_copy(blk, slot, r):
        tok = tok_ref[blk * rb + r]
        return pltpu.make_async_copy(x_hbm.at[pl.ds(tok, 1), :], buf.at[slot, pl.ds(r, 1), :], sem.at[slot])

    def issue(blk, slot):
        nv = nval_ref[blk]

        @pl.when(nv < rb)
        def _():
            buf[slot] = jnp.zeros(buf.shape[1:], buf.dtype)

        def start(r, c):
            row_copy(blk, slot, r).start()
            return c

        lax.fori_loop(0, nv, start, 0)

    def wait(blk, slot):
        def wait_row(r, c):
            row_copy(blk, slot, r).wait()
            return c

        lax.fori_loop(0, nval_ref[blk], wait_row, 0)

    slot = _double_buffered(i, pl.num_programs(0), issue, wait)
    o_ref[...] = buf[slot].astype(o_ref.dtype)


def _moe_gather(x, tok_buf, nvalid, rb):
    t, d = x.shape
    cap = tok_buf.shape[0]
    grid_spec = pltpu.PrefetchScalarGridSpec(
        num_scalar_prefetch=2,
        grid=(cap // rb,),
        in_specs=[pl.BlockSpec(memory_space=pl.ANY)],
        out_specs=pl.BlockSpec((rb, d), lambda i, tok, nv: (i, 0)),
        scratch_shapes=[pltpu.VMEM((2, rb, d), F32), pltpu.SemaphoreType.DMA((2,))],
    )
    return pl.pallas_call(
        functools.partial(_gather_body, rb=rb),
        grid_spec=grid_spec,
        out_shape=jax.ShapeDtypeStruct((cap, d), BF16),
        compiler_params=_cparams(1, 6 * rb * d * 4),
        name="moe_gather",
    )(tok_buf, nvalid, x)


def _fresh_weights(be_ref, nu_ref):
    i = pl.program_id(1)
    prev = be_ref[jnp.maximum(i - 1, 0)]
    used = i < nu_ref[0]
    return used, used & ((i == 0) | (be_ref[i] != prev))


def _moe_gu_body(be_ref, nu_ref, x_ref, wg_ref, wl_ref, bg_ref, bl_ref, h_ref, wgb, wlb):
    used, fresh = _fresh_weights(be_ref, nu_ref)

    @pl.when(fresh)
    def _():
        wgb[...] = wg_ref[...].astype(BF16)
        wlb[...] = wl_ref[...].astype(BF16)

    @pl.when(used)
    def _():
        x = x_ref[...]
        gate = jnp.dot(x, wgb[...], preferred_element_type=F32) + bg_ref[...]
        lin = jnp.dot(x, wlb[...], preferred_element_type=F32) + bl_ref[...]
        gate = jnp.minimum(gate, SWIGLU_LIMIT)
        lin = jnp.clip(lin, -SWIGLU_LIMIT, SWIGLU_LIMIT)
        h_ref[...] = (gate * _sigmoid(SWIGLU_ALPHA * gate) * (lin + 1.0)).astype(h_ref.dtype)

    @pl.when(jnp.logical_not(used))
    def _():
        h_ref[...] = jnp.zeros_like(h_ref)


def _moe_gu(xs, block_e, n_used, w_gu, b_gu, layer, rb):
    cap, d = xs.shape
    de = w_gu.shape[3] // 2
    tn = _tile(de, 256, LANES)
    nj = de // tn
    wmap = lambda off: (lambda j, i, be, nu: (layer, be[i], 0, off + j))
    grid_spec = pltpu.PrefetchScalarGridSpec(
        num_scalar_prefetch=2,
        grid=(nj, cap // rb),
        in_specs=[pl.BlockSpec((rb, d), lambda j, i, be, nu: (i, 0)),
                  pl.BlockSpec((None, None, d, tn), wmap(0)),
                  pl.BlockSpec((None, None, d, tn), wmap(nj)),
                  pl.BlockSpec((None, None, 1, tn), wmap(0)),
                  pl.BlockSpec((None, None, 1, tn), wmap(nj))],
        out_specs=pl.BlockSpec((rb, tn), lambda j, i, be, nu: (i, j)),
        scratch_shapes=[pltpu.VMEM((d, tn), BF16), pltpu.VMEM((d, tn), BF16)],
    )
    b4 = b_gu.reshape(b_gu.shape[0], b_gu.shape[1], 1, b_gu.shape[2])
    return pl.pallas_call(
        _moe_gu_body,
        grid_spec=grid_spec,
        out_shape=jax.ShapeDtypeStruct((cap, de), BF16),
        compiler_params=_cparams(2, 2 * (2 * d * tn * 4) + 2 * d * tn * 2 + 4 * rb * d + 8 * rb * tn * 4 + (4 << 20)),
        name="moe_gu",
    )(block_e, n_used, xs, w_gu, w_gu, b4, b4)


def _moe_down_body(be_ref, nu_ref, h_ref, w_ref, b_ref, y_ref, wb):
    used, fresh = _fresh_weights(be_ref, nu_ref)

    @pl.when(fresh)
    def _():
        wb[...] = w_ref[...].astype(BF16)

    @pl.when(used)
    def _():
        y_ref[...] = jnp.dot(h_ref[...], wb[...], preferred_element_type=F32) + b_ref[...]

    @pl.when(jnp.logical_not(used))
    def _():
        y_ref[...] = jnp.zeros_like(y_ref)


def _moe_down(h, block_e, n_used, w_down, b_down, layer, rb):
    cap, de = h.shape
    d = w_down.shape[3]
    tn = _tile(d, 1024, LANES)
    wmap = lambda j, i, be, nu: (layer, be[i], 0, j)
    grid_spec = pltpu.PrefetchScalarGridSpec(
        num_scalar_prefetch=2,
        grid=(d // tn, cap // rb),
        in_specs=[pl.BlockSpec((rb, de), lambda j, i, be, nu: (i, 0)),
                  pl.BlockSpec((None, None, de, tn), wmap),
                  pl.BlockSpec((None, None, 1, tn), wmap)],
        out_specs=pl.BlockSpec((rb, tn), lambda j, i, be, nu: (i, j)),
        scratch_shapes=[pltpu.VMEM((de, tn), BF16)],
    )
    b4 = b_down.reshape(b_down.shape[0], b_down.shape[1], 1, b_down.shape[2])
    return pl.pallas_call(
        _moe_down_body,
        grid_spec=grid_spec,
        out_shape=jax.ShapeDtypeStruct((cap, d), F32),
        compiler_params=_cparams(2, 2 * de * tn * 4 + de * tn * 2 + 4 * rb * de + 6 * rb * tn * 4 + (4 << 20)),
        name="moe_down",
    )(block_e, n_used, h, w_down, b4)


def _combine_body(pos_ref, y_hbm, x_ref, gt_ref, g_ref, b_ref, xo_ref, xb_ref, buf, sem, *, tb, alpha):
    i = pl.program_id(0)
    n_rows = tb * TOP_K

    def row_copy(blk, slot, n):
        p = pos_ref[blk * n_rows + n]
        return pltpu.make_async_copy(y_hbm.at[pl.ds(p, 1), :],
                                     buf.at[slot, n % TOP_K, pl.ds(n // TOP_K, 1), :], sem.at[slot])

    def issue(blk, slot):
        def start(n, c):
            row_copy(blk, slot, n).start()
            return c

        lax.fori_loop(0, n_rows, start, 0)

    def wait(blk, slot):
        def wait_row(n, c):
            row_copy(blk, slot, n).wait()
            return c

        lax.fori_loop(0, n_rows, wait_row, 0)

    slot = _double_buffered(i, pl.num_programs(0), issue, wait)
    gates = gt_ref[...]
    f = buf[slot, 0] * gates[:, 0:1]
    for k in range(1, TOP_K):
        f = f + buf[slot, k] * gates[:, k:k + 1]
    xn = _layer_norm(alpha * x_ref[...] + f, g_ref[...], b_ref[...])
    xo_ref[...] = xn
    xb_ref[...] = xn.astype(BF16)


def _moe_combine_ln(y_sorted, pos, gates, x, g, b, alpha):
    t, d = x.shape
    tb = _tile(t, 64, SUBLANES)
    row = lambda i, p: (i, 0)
    grid_spec = pltpu.PrefetchScalarGridSpec(
        num_scalar_prefetch=1,
        grid=(t // tb,),
        in_specs=[pl.BlockSpec(memory_space=pl.ANY),
                  pl.BlockSpec((tb, d), row),
                  pl.BlockSpec((tb, LANES), row),
                  pl.BlockSpec((1, d), lambda i, p: (0, 0)),
                  pl.BlockSpec((1, d), lambda i, p: (0, 0))],
        out_specs=[pl.BlockSpec((tb, d), row), pl.BlockSpec((tb, d), row)],
        scratch_shapes=[pltpu.VMEM((2, TOP_K, tb, d), F32), pltpu.SemaphoreType.DMA((2,))],
    )
    return pl.pallas_call(
        functools.partial(_combine_body, tb=tb, alpha=alpha),
        grid_spec=grid_spec,
        out_shape=[jax.ShapeDtypeStruct((t, d), F32), jax.ShapeDtypeStruct((t, d), BF16)],
        compiler_params=_cparams(1, (2 * TOP_K + 10) * tb * d * 4),
        name="moe_combine_ln",
    )(pos, y_sorted, x, gates, g, b)


def _slot_layout(e_idx, rank, counts, rb, gb):
    n_tok = e_idx.shape[0]
    n_experts = counts.shape[0]
    n_assign = n_tok * TOP_K
    padded = (counts + rb - 1) // rb * rb
    pad_ends = jnp.cumsum(padded)
    pad_starts = pad_ends - padded
    pos = (pad_starts[e_idx] + rank).reshape(-1).astype(jnp.int32)
    n_blocks = -(-n_assign // rb) + n_experts
    cap = n_blocks * rb
    tok_buf = jnp.zeros((cap,), jnp.int32).at[pos].set(jnp.arange(n_assign, dtype=jnp.int32) // TOP_K)
    blk0 = jnp.arange(n_blocks, dtype=jnp.int32) * rb
    block_e = jnp.minimum(jnp.searchsorted(pad_ends, blk0, side='right'), n_experts - 1).astype(jnp.int32)
    g0 = jnp.arange(cap // gb, dtype=jnp.int32) * gb
    ge = jnp.repeat(block_e, rb // gb)
    nvalid = jnp.clip(pad_starts[ge] + counts[ge] - g0, 0, gb)
    nvalid = jnp.where(g0 < pad_ends[-1], nvalid, 0).astype(jnp.int32)
    n_used = (pad_ends[-1] // rb).astype(jnp.int32).reshape(1)
    return pos, tok_buf, block_e, nvalid, n_used


def _last_rows(a, row0, n_seq, seq_len, n, width):
    idx = row0 + jnp.arange(n_seq, dtype=jnp.int32)[:, None] * seq_len + (seq_len - n) + jnp.arange(n, dtype=jnp.int32)
    return a[idx.reshape(-1), :width].reshape(n_seq, n, width)


def kernel(x_prompt, x_sample, state_s5_re, state_s5_im, state_gdn, state_gdn_conv, ln_mix_g, ln_mix_b, ln_ffn_g, ln_ffn_b, s5_a_re, s5_a_im, s5_log_step, s5_b_re, s5_b_im, s5_c_re, s5_c_im, s5_d, s5_w_val, s5_w_gate, gdn_w_in, gdn_conv_w, gdn_a_log, gdn_dt_bias, gdn_norm_w, gdn_w_out, moe_router_w, moe_router_b, moe_w_gu, moe_b_gu, moe_w_down, moe_b_down):
    nbp, sp, d = x_prompt.shape
    nbs, ss, _ = x_sample.shape
    depth = ln_mix_g.shape[0]
    n_p = nbp * sp
    n_s = nbs * ss
    n_experts = moe_router_w.shape[2]
    alpha = (2 * depth) ** 0.25
    gp = state_s5_re.shape[2] * state_s5_re.shape[3]
    n_vh, dk, dv = state_gdn.shape[2], state_gdn.shape[3], state_gdn.shape[4]
    conv_dim = state_gdn_conv.shape[3]
    v_dim = n_vh * dv
    n_tail = state_gdn_conv.shape[2]
    rb = MOE_ROW_BLOCK
    gb = MOE_GATHER_BLOCK

    x = jnp.concatenate([x_prompt.reshape(n_p, d), x_sample.reshape(n_s, d)], axis=0)
    xb = x.astype(BF16)
    s5_h0r = state_s5_re.reshape(state_s5_re.shape[0], nbs, gp)
    s5_h0i = state_s5_im.reshape(state_s5_im.shape[0], nbs, gp)
    rw_pad = jnp.pad(moe_router_w, ((0, 0), (0, 0), (0, LANES - n_experts)))
    rb_pad = jnp.pad(moe_router_b, ((0, 0), (0, LANES - n_experts))).reshape(depth, 1, LANES)

    s5_re_p, s5_im_p, s5_re_s, s5_im_s = [], [], [], []
    gdn_p, conv_p, gdn_s, conv_s = [], [], [], []
    for i in range(depth):
        j = i // 2
        if i % 2 == 0:
            prm = _s5_prepare(s5_a_re[j], s5_a_im[j], s5_log_step[j], s5_b_re[j], s5_b_im[j], s5_c_re[j], s5_c_im[j])
            zeros = jnp.zeros((nbp, gp), F32)
            z, hr, hi = _s5_scan(x, 0, nbp, sp, prm, s5_d[j], zeros, zeros, jnp.zeros((n_p + n_s, d), BF16))
            s5_re_p.append(hr)
            s5_im_p.append(hi)
            z, hr, hi = _s5_scan(x, n_p, nbs, ss, prm, s5_d[j], s5_h0r[j], s5_h0i[j], z)
            s5_re_s.append(hr)
            s5_im_s.append(hi)
            mix = _matmul(z, [s5_w_val, s5_w_gate], j, 0, d, _gated_epilogue, F32, tn_target=256, name="s5_out")
        else:
            proj = _matmul(xb, [gdn_w_in], j, 0, conv_dim + v_dim, _identity_epilogue, F32, name="gdn_in")
            ba = _matmul(xb, [gdn_w_in], j, conv_dim + v_dim, 2 * n_vh, _identity_epilogue, F32, name="gdn_in_ba")
            pad = jnp.zeros((n_vh,), F32)
            acoef = jnp.concatenate([pad, -jnp.exp(gdn_a_log[j])]).reshape(1, LANES)
            dtb = jnp.concatenate([pad, gdn_dt_bias[j]]).reshape(1, LANES)
            nw = gdn_norm_w[j].reshape(1, dv)
            c_zero = jnp.zeros((1, nbp, n_tail, conv_dim), F32)
            s_zero = jnp.zeros((1, nbp, n_vh, dk, dv), F32)
            o, s_new = _gdn_core(proj, ba, 0, nbp, sp, gdn_conv_w[j], c_zero, 0, acoef, dtb, nw, s_zero, 0,
                                 jnp.zeros((n_p + n_s, v_dim), BF16))
            gdn_p.append(s_new)
            o, s_new = _gdn_core(proj, ba, n_p, nbs, ss, gdn_conv_w[j], state_gdn_conv, j, acoef, dtb, nw,
                                 state_gdn, j, o)
            gdn_s.append(s_new)
            conv_p.append(_last_rows(proj, 0, nbp, sp, n_tail, conv_dim))
            conv_s.append(_last_rows(proj, n_p, nbs, ss, n_tail, conv_dim))
            mix = _matmul(o, [gdn_w_out], j, 0, d, _identity_epilogue, F32, tn_target=256, tm_target=512,
                          name="gdn_out")
        x, e_out, r_out, g_out, cnt = _ln_router(x, mix, ln_mix_g[i].reshape(1, d), ln_mix_b[i].reshape(1, d),
                                                 rw_pad[i], rb_pad[i], alpha, n_experts)
        pos, tok_buf, block_e, nvalid, n_used = _slot_layout(
            e_out[:, :TOP_K], r_out[:, :TOP_K], cnt[0, :n_experts].astype(jnp.int32), rb, gb)
        xs = _moe_gather(x, tok_buf, nvalid, gb)
        h = _moe_gu(xs, block_e, n_used, moe_w_gu, moe_b_gu, i, rb)
        y = _moe_down(h, block_e, n_used, moe_w_down, moe_b_down, i, rb)
        x, xb = _moe_combine_ln(y, pos, g_out, x, ln_ffn_g[i].reshape(1, d), ln_ffn_b[i].reshape(1, d), alpha)

    gs = state_s5_re.shape[2:]
    st = lambda lst, nb: jnp.stack(lst).reshape((len(lst), nb) + gs)
    return (x[:n_p].reshape(nbp, sp, d), x[n_p:].reshape(nbs, ss, d),
            st(s5_re_p, nbp), st(s5_im_p, nbp), jnp.stack(gdn_p), jnp.stack(conv_p),
            st(s5_re_s, nbs), st(s5_im_s, nbs), jnp.stack(gdn_s), jnp.stack(conv_s))
```

```python
import functools
import math

import jax
import jax.numpy as jnp
from jax import lax
from jax.experimental import pallas as pl
from jax.experimental.pallas import tpu as pltpu

F32 = jnp.float32
BF16 = jnp.bfloat16

TOP_K = 4
SWIGLU_ALPHA = 1.702
SWIGLU_LIMIT = 7.0
LN_EPS = 1e-5
RMS_EPS = 1e-6
L2_EPS = 1e-6

LANES = 128
SUBLANES = 8
VMEM_CAP = 56 << 20

S5_GROUP_BLOCK = 16
GDN_CHUNK = 64
GDN_INV_BLOCK = 16
MOE_ROW_BLOCK = 512
MOE_GATHER_BLOCK = 256
DMA_UNROLL = 8


def _tile(dim, target, align):
    t = min(target, dim) // align * align
    while t >= align:
        if dim % t == 0:
            return t
        t -= align
    return dim


def _cparams(n_axes, vmem_bytes):
    return pltpu.CompilerParams(
        dimension_semantics=("arbitrary",) * n_axes,
        vmem_limit_bytes=int(min(max(vmem_bytes, 16 << 20), VMEM_CAP)))


def _sigmoid(x):
    return 1.0 / (1.0 + jnp.exp(-x))


def _silu(x):
    return x * _sigmoid(x)


def _layer_norm(y, g, b):
    mu = jnp.mean(y, axis=-1, keepdims=True)
    yc = y - mu
    var = jnp.mean(yc * yc, axis=-1, keepdims=True)
    return yc * lax.rsqrt(var + LN_EPS) * g + b


def _mm_body(x_ref, *refs, n_w, epilogue):
    w_refs = refs[:n_w]
    o_ref = refs[n_w]
    wb_refs = refs[n_w + 1:]

    @pl.when(pl.program_id(1) == 0)
    def _():
        for w, wb in zip(w_refs, wb_refs):
            wb[...] = w[...].astype(BF16)

    x = x_ref[...]
    accs = [jnp.dot(x, wb[...], preferred_element_type=F32) for wb in wb_refs]
    o_ref[...] = epilogue(*accs).astype(o_ref.dtype)


def _matmul(x, ws, layer, col0, n_cols, epilogue, out_dtype, tn_target=512, tm_target=1024, name="mm"):
    m, k = x.shape
    tn = _tile(n_cols, tn_target, LANES)
    tm = _tile(m, tm_target, SUBLANES)
    assert col0 % tn == 0
    c0 = col0 // tn
    n_w = len(ws)
    w_spec = pl.BlockSpec((None, k, tn), lambda j, i: (layer, 0, c0 + j))
    vmem = 2 * tm * k * 2 + n_w * (2 * k * tn * 4 + k * tn * 2) + 2 * tm * tn * 4 + (n_w + 1) * tm * tn * 4
    return pl.pallas_call(
        functools.partial(_mm_body, n_w=n_w, epilogue=epilogue),
        grid=(n_cols // tn, m // tm),
        in_specs=[pl.BlockSpec((tm, k), lambda j, i: (i, 0))] + [w_spec] * n_w,
        out_specs=pl.BlockSpec((tm, tn), lambda j, i: (i, j)),
        out_shape=jax.ShapeDtypeStruct((m, n_cols), out_dtype),
        scratch_shapes=[pltpu.VMEM((k, tn), BF16)] * n_w,
        compiler_params=_cparams(2, vmem + (4 << 20)),
        name=name,
    )(x, *ws)


def _gated_epilogue(val, gate):
    return val * _sigmoid(gate)


def _identity_epilogue(acc):
    return acc


def _ln_router_body(*refs, alpha, n_experts, first_blocks):
    x_refs = refs[:len(first_blocks)]
    m_ref, g_ref, b_ref, rw_ref, rb_ref, xo_ref, eo_ref, ro_ref, go_ref, cnt_ref = refs[len(first_blocks):]
    x = x_refs[0][...]
    for p in range(1, len(x_refs)):
        x = jnp.where(pl.program_id(0) >= first_blocks[p], x_refs[p][...], x)
    xn = _layer_norm(alpha * x + m_ref[...], g_ref[...], b_ref[...])
    xo_ref[...] = xn
    tm = xn.shape[0]
    lg = jnp.dot(xn.astype(BF16), rw_ref[...], preferred_element_type=F32) + rb_ref[...]
    lane = lax.broadcasted_iota(jnp.int32, (tm, LANES), 1)
    lane_f = lane.astype(F32)
    lg = jnp.where(lane < n_experts, lg, -jnp.inf)

    @pl.when(pl.program_id(0) == 0)
    def _():
        cnt_ref[...] = jnp.zeros_like(cnt_ref)

    vals, hots = [], []
    e_out = jnp.zeros((tm, LANES), jnp.int32)
    for k in range(TOP_K):
        m = jnp.max(lg, axis=1, keepdims=True)
        idx = jnp.min(jnp.where(lg == m, lane_f, float(LANES)), axis=1, keepdims=True)
        hot = lane_f == idx
        vals.append(m)
        hots.append(hot)
        lg = jnp.where(hot, -jnp.inf, lg)
        e_out = jnp.where(lane == k, idx.astype(jnp.int32), e_out)
    eo_ref[...] = e_out

    ex = [jnp.exp(v - vals[0]) for v in vals]
    den = ex[0]
    for e in ex[1:]:
        den = den + e
    g_out = jnp.zeros((tm, LANES), F32)
    for k in range(TOP_K):
        g_out = jnp.where(lane == k, ex[k] / den, g_out)
    go_ref[...] = g_out

    hot_all = hots[0].astype(F32)
    for h in hots[1:]:
        hot_all = hot_all + h.astype(F32)
    ri = lax.broadcasted_iota(jnp.int32, (tm, tm), 0)
    ci = lax.broadcasted_iota(jnp.int32, (tm, tm), 1)
    before = jnp.dot((ri > ci).astype(BF16), hot_all.astype(BF16), preferred_element_type=F32) + cnt_ref[...]
    r_out = jnp.zeros((tm, LANES), jnp.int32)
    for k in range(TOP_K):
        rank = jnp.sum(jnp.where(hots[k], before, 0.0), axis=1, keepdims=True)
        r_out = jnp.where(lane == k, rank.astype(jnp.int32), r_out)
    ro_ref[...] = r_out
    cnt_ref[...] += jnp.sum(hot_all, axis=0, keepdims=True)


def _ln_router(xs, m, g, b, rw, rb, alpha, n_experts):
    t, d = m.shape
    tm = _tile(math.gcd(*[a.shape[0] for a in xs]), 256, SUBLANES)
    first_blocks, x_specs, blk = [], [], 0
    for a in xs:
        nb = a.shape[0] // tm
        first_blocks.append(blk)
        x_specs.append(pl.BlockSpec((tm, d), lambda i, blk=blk, nb=nb: (jnp.clip(i - blk, 0, nb - 1), 0)))
        blk += nb
    assert blk * tm == t
    row = pl.BlockSpec((tm, d), lambda i: (i, 0))
    vec = pl.BlockSpec((1, d), lambda i: (0, 0))
    small = pl.BlockSpec((tm, LANES), lambda i: (i, 0))
    one = pl.BlockSpec((1, LANES), lambda i: (0, 0))
    return pl.pallas_call(
        functools.partial(_ln_router_body, alpha=alpha, n_experts=n_experts, first_blocks=tuple(first_blocks)),
        grid=(t // tm,),
        in_specs=x_specs + [row, vec, vec, pl.BlockSpec((d, LANES), lambda i: (0, 0)), one],
        out_specs=[row, small, small, small, one],
        out_shape=[jax.ShapeDtypeStruct((t, d), F32), jax.ShapeDtypeStruct((t, LANES), jnp.int32),
                   jax.ShapeDtypeStruct((t, LANES), jnp.int32), jax.ShapeDtypeStruct((t, LANES), F32),
                   jax.ShapeDtypeStruct((1, LANES), F32)],
        compiler_params=_cparams(1, (8 + 2 * len(xs)) * tm * d * 4 + 4 * d * LANES * 4),
        name="ln_router",
    )(*xs, m, g, b, rw, rb)


def _s5_body(x_ref, wb_ref, wc_ref, tab_ref, d_ref, h0r_ref, h0i_ref,
             z_ref, hlr_ref, hli_ref, hre, him, car, *, rows, seq_len, nc, time_sliced):
    u = x_ref[...]
    bu = jnp.dot(u.astype(BF16), wb_ref[...], preferred_element_type=F32)
    hre[...] = bu[:, :nc]
    him[...] = bu[:, nc:]

    def scan8(i, carry):
        sl = pl.ds(pl.multiple_of(i * SUBLANES, SUBLANES), SUBLANES)
        xr = hre[sl, :]
        xi = him[sl, :]
        for n, d in enumerate((1, 2, 4)):
            ar = tab_ref[2 * n]
            ai = tab_ref[2 * n + 1]
            sr = pltpu.roll(xr, d, 0)
            si = pltpu.roll(xi, d, 0)
            xr, xi = xr + (ar * sr - ai * si), xi + (ar * si + ai * sr)
        pr = tab_ref[6]
        pi_ = tab_ref[7]
        cr, ci = carry
        xr = xr + (pr * cr - pi_ * ci)
        xi = xi + (pr * ci + pi_ * cr)
        hre[sl, :] = xr
        him[sl, :] = xi
        return xr[SUBLANES - 1:SUBLANES, :], xi[SUBLANES - 1:SUBLANES, :]

    if time_sliced:
        b = pl.program_id(1)
        t = pl.program_id(2)

        @pl.when(t == 0)
        def _():
            car[0:1, :] = h0r_ref[pl.ds(b, 1), :]
            car[1:2, :] = h0i_ref[pl.ds(b, 1), :]

        cr, ci = lax.fori_loop(0, rows // SUBLANES, scan8, (car[0:1, :], car[1:2, :]))
        car[0:1, :] = cr
        car[1:2, :] = ci

        @pl.when(t == pl.num_programs(2) - 1)
        def _():
            hlr_ref[pl.ds(b, 1), :] = cr
            hli_ref[pl.ds(b, 1), :] = ci
    else:
        per_seq = seq_len // SUBLANES

        def one_seq(s, _):
            carry = (h0r_ref[pl.ds(s, 1), :], h0i_ref[pl.ds(s, 1), :])
            cr, ci = lax.fori_loop(s * per_seq, (s + 1) * per_seq, scan8, carry)
            hlr_ref[pl.ds(s, 1), :] = cr
            hli_ref[pl.ds(s, 1), :] = ci
            return 0

        lax.fori_loop(0, rows // seq_len, one_seq, 0)

    hcat = jnp.concatenate([hre[...].astype(BF16), him[...].astype(BF16)], axis=1)
    y = jnp.dot(hcat, wc_ref[...], preferred_element_type=F32) + d_ref[...] * u
    z = 0.5 * y * (1.0 + lax.erf(y * (1.0 / math.sqrt(2.0))))
    z_ref[...] = z.astype(z_ref.dtype)


def _s5_prepare(a_re, a_im, log_step, b_re, b_im, c_re, c_im):
    g, p = a_re.shape
    cg = b_re.shape[-1]
    gb = min(S5_GROUP_BLOCK, g)
    nblk = g // gb
    step = jnp.exp(log_step)
    zr, zi = a_re * step, a_im * step

    def power(k):
        mag = jnp.exp(k * zr)
        return mag * jnp.cos(k * zi), mag * jnp.sin(k * zi)

    lbr, lbi = power(1.0)
    den = a_re * a_re + a_im * a_im
    nr, ni = lbr - 1.0, lbi
    cf_r = (nr * a_re + ni * a_im) / den
    cf_i = (ni * a_re - nr * a_im) / den
    bb_r = cf_r[..., None] * b_re - cf_i[..., None] * b_im
    bb_i = cf_r[..., None] * b_im + cf_i[..., None] * b_re
    eye = jnp.eye(gb, dtype=F32)

    def blockdiag_in(m):
        m = m.reshape(nblk, gb, p, cg)
        return jnp.einsum('agpc,gh->agchp', m, eye).reshape(nblk, gb * cg, gb * p)

    def blockdiag_out(m):
        m = m.reshape(nblk, gb, cg, p)
        return jnp.einsum('agcp,gh->agphc', m, eye).reshape(nblk, gb * p, gb * cg)

    wb = jnp.concatenate([blockdiag_in(bb_r), blockdiag_in(bb_i)], axis=2).astype(BF16)
    wc = jnp.concatenate([blockdiag_out(c_re), blockdiag_out(-c_im)], axis=1).astype(BF16)

    rows = jnp.arange(SUBLANES, dtype=F32)[:, None]
    tabs = []
    for d in (1, 2, 4):
        pr, pi_ = power(float(d))
        keep = (rows >= d).astype(F32)
        tabs += [keep * pr.reshape(nblk, 1, gb * p), keep * pi_.reshape(nblk, 1, gb * p)]
    kk = (rows + 1.0)[None]
    mag = jnp.exp(kk * zr.reshape(nblk, 1, gb * p))
    ang = kk * zi.reshape(nblk, 1, gb * p)
    tabs += [mag * jnp.cos(ang), mag * jnp.sin(ang)]
    tab = jnp.stack(tabs, axis=1)
    return wb, wc, tab


def _s5_scan(x_src, x_row0, row0, n_seq, seq_len, prm, d_skip, h0r, h0i, z_prev):
    wb, wc, tab = prm
    t_all, d = z_prev.shape
    nblk, kin, n2 = wb.shape
    nc = n2 // 2
    time_sliced = seq_len >= 512
    if time_sliced:
        rows = _tile(seq_len, 512, SUBLANES)
        n_t = seq_len // rows
        grid = (nblk, n_seq, n_t)
        nb_blk = n_seq
        row_map = lambda base: (lambda g, b, t: (base // rows + b * n_t + t, g))
        st_map = lambda g, b, t: (0, g)
    else:
        seqs = _tile(n_seq, max(1024 // seq_len, 1), SUBLANES)
        rows = seqs * seq_len
        grid = (nblk, n_seq // seqs, 1)
        nb_blk = seqs
        row_map = lambda base: (lambda g, b, t: (base // rows + b, g))
        st_map = lambda g, b, t: (b, g)
    assert row0 % rows == 0 and x_row0 % rows == 0
    x_spec = pl.BlockSpec((rows, kin), row_map(x_row0))
    z_spec = pl.BlockSpec((rows, kin), row_map(row0))
    st_spec = pl.BlockSpec((nb_blk, nc), st_map)
    body = functools.partial(_s5_body, rows=rows, seq_len=seq_len, nc=nc, time_sliced=time_sliced)
    in_specs = [x_spec,
                pl.BlockSpec((None, kin, n2), lambda g, b, t: (g, 0, 0)),
                pl.BlockSpec((None, n2, kin), lambda g, b, t: (g, 0, 0)),
                pl.BlockSpec((None, 8, SUBLANES, nc), lambda g, b, t: (g, 0, 0, 0)),
                pl.BlockSpec((1, kin), lambda g, b, t: (0, g)),
                st_spec, st_spec]
    args = [x_src, wb, wc, tab, d_skip.reshape(1, d), h0r, h0i]
    n_in = len(args)
    in_specs.append(pl.BlockSpec(memory_space=pl.ANY))
    args.append(z_prev)
    return pl.pallas_call(
        lambda *refs: body(*refs[:n_in], *refs[n_in + 1:]),
        grid=grid,
        in_specs=in_specs,
        out_specs=[z_spec, st_spec, st_spec],
        out_shape=[jax.ShapeDtypeStruct((t_all, d), BF16),
                   jax.ShapeDtypeStruct(h0r.shape, F32), jax.ShapeDtypeStruct(h0i.shape, F32)],
        scratch_shapes=[pltpu.VMEM((rows, nc), F32), pltpu.VMEM((rows, nc), F32), pltpu.VMEM((SUBLANES, nc), F32)],
        input_output_aliases={n_in: 0},
        compiler_params=_cparams(3, 8 * rows * n2 * 4 + 8 * kin * n2 * 2),
        name="s5_scan",
    )(*args)


def _bdot(a, b):
    return jnp.dot(a.astype(BF16), b.astype(BF16), preferred_element_type=F32)


def _select_dot(sel, x, dims):
    hi = x.astype(BF16)
    r1 = x - hi.astype(F32)
    mid = r1.astype(BF16)
    lo = (r1 - mid.astype(F32)).astype(BF16)
    out = lax.dot_general(sel, hi, dims, preferred_element_type=F32)
    out = out + lax.dot_general(sel, mid, dims, preferred_element_type=F32)
    return out + lax.dot_general(sel, lo, dims, preferred_element_type=F32)


def _unit_lower_inverses(lows, ri, ci):
    c = lows[0].shape[0]
    blk = min(GDN_INV_BLOCK, c)
    eye = (ri == ci).astype(F32)
    diag = ri // blk == ci // blk
    a_s = [jnp.where(diag, -low, 0.0) for low in lows]
    t_s = [eye + a for a in a_s]
    k = 1
    while 2 * k < blk:
        a_s = [_bdot(a, a) for a in a_s]
        t_s = [t + _bdot(t, a) for t, a in zip(t_s, a_s)]
        k *= 2
    while blk < c:
        pair = (ri // (2 * blk) == ci // (2 * blk)) & (ri // blk != ci // blk)
        mids = [_bdot(jnp.where(pair, low, 0.0), t) for low, t in zip(lows, t_s)]
        t_s = [t - _bdot(t, m) for t, m in zip(t_s, mids)]
        blk *= 2
    return t_s


def _gdn_body(q_ref, k_ref, v_ref, z_ref, ba_ref, cwq_ref, cwk_ref, cwv_ref, c0q_ref, c0k_ref, c0v_ref,
              acoef_ref, dtb_ref, nw_ref, s0_ref, o_ref, sn_ref,
              xq, xk, xv, qs, ks, vs, *, rows, seg_rows, carried, n_vh, dk, dv):
    hp = pl.program_id(1)
    halo = SUBLANES - 3
    slot = SUBLANES + seg_rows
    n_seg = rows // seg_rows
    c = GDN_CHUNK
    sub = min(seg_rows, c)
    n_sub = c // sub

    def init():
        for s in range(n_seg):
            xq[s * slot + halo:s * slot + SUBLANES, :] = c0q_ref[s]
            xk[s * slot + halo:s * slot + SUBLANES, :] = c0k_ref[s]
            xv[s * slot + halo:s * slot + SUBLANES, :] = c0v_ref[s]
        sn_ref[...] = s0_ref[...]

    if carried:
        pl.when(pl.program_id(2) == 0)(init)
    else:
        init()

    def conv(x_scr, blk_ref, cw_ref, out_ref, post):
        for s in range(n_seg):
            base = s * slot
            x_scr[base + SUBLANES:base + slot, :] = blk_ref[s * seg_rows:(s + 1) * seg_rows, :]
            y = cw_ref[0:1, :] * x_scr[base + halo:base + halo + seg_rows, :]
            for i in range(1, 4):
                y = y + cw_ref[i:i + 1, :] * x_scr[base + halo + i:base + halo + i + seg_rows, :]
            if carried:
                x_scr[base + halo:base + SUBLANES, :] = x_scr[base + seg_rows + halo:base + slot, :]
            out_ref[s * seg_rows:(s + 1) * seg_rows, :] = post(_silu(y))

    def l2n(x):
        return x * lax.rsqrt(jnp.sum(x * x, axis=-1, keepdims=True) + L2_EPS)

    conv(xq, q_ref, cwq_ref, qs, lambda y: l2n(y) * (dk ** -0.5))
    conv(xk, k_ref, cwk_ref, ks, l2n)
    conv(xv, v_ref, cwv_ref, vs, lambda y: y)

    ba = ba_ref[...]
    beta_all = _sigmoid(ba)
    sp = ba + dtb_ref[...]
    g_all = acoef_ref[...] * (jnp.maximum(sp, 0.0) + jnp.log1p(jnp.exp(-jnp.abs(sp))))
    rr = lax.broadcasted_iota(jnp.int32, (rows, rows), 0)
    cc = lax.broadcasted_iota(jnp.int32, (rows, rows), 1)
    run = ((rr // sub == cc // sub) & (rr >= cc)).astype(BF16)
    mm_dims = (((1,), (0,)), ((), ()))
    gcs = _select_dot(run, g_all, mm_dims)

    ri = lax.broadcasted_iota(jnp.int32, (c, c), 0)
    ci = lax.broadcasted_iota(jnp.int32, (c, c), 1)
    same = ri // sub == ci // sub
    causal = same & (ri >= ci)
    strict = same & (ri > ci)
    lane = lax.broadcasted_iota(jnp.int32, (c, LANES), 1)
    sel_rows = lax.broadcasted_iota(jnp.int32, (SUBLANES, LANES), 0)
    sel_lane = lax.broadcasted_iota(jnp.int32, (SUBLANES, LANES), 1)
    nt_dims = (((1,), (1,)), ((), ()))
    tn_dims = (((0,), (0,)), ((), ()))
    n_chunks = rows // c

    chunks = range(n_chunks)
    probs = [(ic, hh) for ic in chunks for hh in range(2)]
    qc = [qs[ic * c:(ic + 1) * c, :] for ic in chunks]
    kc = [ks[ic * c:(ic + 1) * c, :] for ic in chunks]
    gcc = [gcs[ic * c:(ic + 1) * c, :] for ic in chunks]
    bcc = [beta_all[ic * c:(ic + 1) * c, :] for ic in chunks]
    kq = [lax.dot_general(jnp.concatenate([qc[ic], kc[ic]], axis=0).astype(BF16), kc[ic].astype(BF16), nt_dims,
                          preferred_element_type=F32) for ic in chunks]
    pick = ((sel_rows < 2) & (sel_lane == n_vh + 2 * hp + sel_rows)).astype(BF16)
    grows = [_select_dot(pick, gcc[ic], nt_dims) for ic in chunks]
    beta, gcol, decay, lows = {}, {}, {}, []
    for ic, hh in probs:
        beta[ic, hh] = jnp.sum(jnp.where(lane == 2 * hp + hh, bcc[ic], 0.0), axis=1, keepdims=True)
        gcol[ic, hh] = jnp.sum(jnp.where(lane == n_vh + 2 * hp + hh, gcc[ic], 0.0), axis=1, keepdims=True)
        decay[ic, hh] = jnp.exp(jnp.where(causal, gcol[ic, hh] - grows[ic][hh:hh + 1, :], -jnp.inf))
        lows.append(jnp.where(strict, kq[ic][c:, :] * beta[ic, hh] * decay[ic, hh], 0.0))
    tmats = _unit_lower_inverses(lows, ri, ci)
    u, w, qkm, qe = {}, {}, {}, {}
    for (ic, hh), tmat in zip(probs, tmats):
        eg = jnp.exp(gcol[ic, hh])
        vb = vs[ic * c:(ic + 1) * c, hh * dv:(hh + 1) * dv] * beta[ic, hh]
        uw = _bdot(tmat, jnp.concatenate([vb, kc[ic] * (beta[ic, hh] * eg)], axis=1))
        u[ic, hh] = uw[:, :dv]
        w[ic, hh] = uw[:, dv:]
        qkm[ic, hh] = kq[ic][:c, :] * decay[ic, hh]
        qe[ic, hh] = qc[ic] * eg

    state = {hh: sn_ref[0, hh] for hh in range(2)} if carried else None
    for ic in chunks:
        items = [(hh, g) for hh in range(2) for g in range(n_sub)]
        span = {g: (g * sub, (g + 1) * sub) for g in range(n_sub)}
        s_old = {(hh, g): state[hh] if carried else sn_ref[ic * n_sub + g, hh] for hh, g in items}
        wq = {(hh, g): _bdot(jnp.concatenate([w[ic, hh][span[g][0]:span[g][1], :],
                                              qe[ic, hh][span[g][0]:span[g][1], :]], axis=0), s_old[hh, g])
              for hh, g in items}
        vn = {(hh, g): u[ic, hh][span[g][0]:span[g][1], :] - wq[hh, g][:sub, :] for hh, g in items}
        for hh, g in items:
            a0, a1 = span[g]
            g_last = gcol[ic, hh][a1 - 1:a1, :]
            kdec = kc[ic][a0:a1, :] * jnp.exp(g_last - gcol[ic, hh][a0:a1, :])
            s_new = s_old[hh, g] * jnp.exp(g_last) + lax.dot_general(
                kdec.astype(BF16), vn[hh, g].astype(BF16), tn_dims, preferred_element_type=F32)
            if carried:
                state[hh] = s_new
            else:
                sn_ref[ic * n_sub + g, hh] = s_new
        for hh in range(2):
            v_new = vn[hh, 0] if n_sub == 1 else jnp.concatenate([vn[hh, g] for g in range(n_sub)], axis=0)
            q_s = wq[hh, 0][sub:, :] if n_sub == 1 else jnp.concatenate(
                [wq[hh, g][sub:, :] for g in range(n_sub)], axis=0)
            o = q_s + _bdot(qkm[ic, hh], v_new)
            o = o * lax.rsqrt(jnp.mean(o * o, axis=-1, keepdims=True) + RMS_EPS) * nw_ref[...]
            o = o * _silu(z_ref[ic * c:(ic + 1) * c, hh * dv:(hh + 1) * dv])
            o_ref[ic * c:(ic + 1) * c, hh * dv:(hh + 1) * dv] = o.astype(o_ref.dtype)
    if carried:
        for hh in range(2):
            sn_ref[0, hh] = state[hh]


def _gdn_core(proj, ba, row0, n_seq, seq_len, conv_w, conv0, conv_idx, acoef, dtb, norm_w, s0, s0_idx, o_prev):
    t_all = proj.shape[0]
    n_vh, dk, dv = s0.shape[2], s0.shape[3], s0.shape[4]
    n_kh = n_vh // 2
    k_dim = n_kh * dk
    v_dim = n_vh * dv
    carried = seq_len >= GDN_CHUNK
    if carried:
        seg_rows = rows = _tile(seq_len, 4 * GDN_CHUNK, GDN_CHUNK)
        sb = 1
    else:
        per_chunk = GDN_CHUNK // seq_len
        sb = _tile(n_seq, 2 * per_chunk, per_chunk)
        seg_rows = seq_len
        rows = sb * seq_len
    n_t = seq_len // seg_rows
    assert rows % GDN_CHUNK == 0 and row0 % rows == 0 and n_seq % sb == 0
    assert dk == LANES and dv == LANES and 2 * n_vh == LANES and seq_len >= 3
    r0 = row0 // rows
    kb = k_dim // dk
    vb_off = 2 * k_dim // (2 * dv)
    zb_off = (2 * k_dim + v_dim) // (2 * dv)
    rmap = lambda off: (lambda b, h, t: (r0 + b * n_t + t, off + h))
    cmap = lambda off: (lambda b, h, t: (0, off + h))
    c0map = lambda off: (lambda b, h, t: (conv_idx, b, 0, off + h))
    vec = pl.BlockSpec((1, LANES), lambda b, h, t: (0, 0))
    in_specs = [
        pl.BlockSpec((rows, dk), rmap(0)),
        pl.BlockSpec((rows, dk), rmap(kb)),
        pl.BlockSpec((rows, 2 * dv), rmap(vb_off)),
        pl.BlockSpec((rows, 2 * dv), rmap(zb_off)),
        pl.BlockSpec((rows, LANES), lambda b, h, t: (r0 + b * n_t + t, 0)),
        pl.BlockSpec((4, dk), cmap(0)),
        pl.BlockSpec((4, dk), cmap(kb)),
        pl.BlockSpec((4, 2 * dv), cmap(vb_off)),
        pl.BlockSpec((None, sb, 3, dk), c0map(0)),
        pl.BlockSpec((None, sb, 3, dk), c0map(kb)),
        pl.BlockSpec((None, sb, 3, 2 * dv), c0map(vb_off)),
        vec, vec, vec,
        pl.BlockSpec((None, sb, 2, dk, dv), lambda b, h, t: (s0_idx, b, h, 0, 0)),
    ]
    args = [proj, proj, proj, proj, ba, conv_w, conv_w, conv_w, conv0, conv0, conv0, acoef, dtb, norm_w, s0]
    o_spec = pl.BlockSpec((rows, 2 * dv), lambda b, h, t: (r0 + b * n_t + t, h))
    body = functools.partial(_gdn_body, rows=rows, seg_rows=seg_rows, carried=carried, n_vh=n_vh, dk=dk, dv=dv)
    n_in = len(args)
    in_specs.append(pl.BlockSpec(memory_space=pl.ANY))
    args.append(o_prev)
    xrows = (rows // seg_rows) * (SUBLANES + seg_rows)
    return pl.pallas_call(
        lambda *refs: body(*refs[:n_in], *refs[n_in + 1:]),
        grid=(n_seq // sb, n_kh, n_t),
        in_specs=in_specs,
        out_specs=[o_spec, pl.BlockSpec((sb, 2, dk, dv), lambda b, h, t: (b, h, 0, 0))],
        out_shape=[jax.ShapeDtypeStruct((t_all, v_dim), BF16),
                   jax.ShapeDtypeStruct((n_seq, n_vh, dk, dv), F32)],
        scratch_shapes=[pltpu.VMEM((xrows, dk), F32), pltpu.VMEM((xrows, dk), F32),
                        pltpu.VMEM((xrows, 2 * dv), F32),
                        pltpu.VMEM((rows, dk), F32), pltpu.VMEM((rows, dk), F32), pltpu.VMEM((rows, 2 * dv), F32)],
        input_output_aliases={n_in: 0},
        compiler_params=_cparams(3, 48 * rows * LANES * 4 + 8 * sb * 2 * dk * dv * 4 + (8 << 20)),
        name="gdn_core",
    )(*args)


def _double_buffered(i, n, issue, wait):
    slot = i % 2

    @pl.when(i == 0)
    def _():
        issue(i, slot)

    @pl.when(i + 1 < n)
    def _():
        issue(i + 1, 1 - slot)

    wait(i, slot)
    return slot


def _gather_body(tok_ref, nval_ref, x_hbm, o_ref, buf, sem, *, rb):
    i = pl.program_id(0)

    def row_copy(blk, slot, r):
        tok = tok_ref[blk * rb + r]
        return pltpu.make_async_copy(x_hbm.at[pl.ds(tok, 1), :], buf.at[slot, pl.ds(r, 1), :], sem.at[slot])

    def issue(blk, slot):
        def start(r, c):
            row_copy(blk, slot, r).start()
            return c

        @pl.when(nval_ref[blk] > 0)
        def _():
            lax.fori_loop(0, rb, start, 0, unroll=DMA_UNROLL)

    def wait(blk, slot):
        def wait_row(r, c):
            row_copy(blk, slot, r).wait()
            return c

        @pl.when(nval_ref[blk] > 0)
        def _():
            lax.fori_loop(0, rb, wait_row, 0, unroll=DMA_UNROLL)

    slot = _double_buffered(i, pl.num_programs(0), issue, wait)

    @pl.when(nval_ref[i] > 0)
    def _():
        o_ref[...] = buf[slot].astype(o_ref.dtype)

    @pl.when(nval_ref[i] == 0)
    def _():
        o_ref[...] = jnp.zeros_like(o_ref)


def _moe_gather(x, tok_buf, nvalid, rb):
    t, d = x.shape
    cap = tok_buf.shape[0]
    grid_spec = pltpu.PrefetchScalarGridSpec(
        num_scalar_prefetch=2,
        grid=(cap // rb,),
        in_specs=[pl.BlockSpec(memory_space=pl.ANY)],
        out_specs=pl.BlockSpec((rb, d), lambda i, tok, nv: (i, 0)),
        scratch_shapes=[pltpu.VMEM((2, rb, d), F32), pltpu.SemaphoreType.DMA((2,))],
    )
    return pl.pallas_call(
        functools.partial(_gather_body, rb=rb),
        grid_spec=grid_spec,
        out_shape=jax.ShapeDtypeStruct((cap, d), BF16),
        compiler_params=_cparams(1, 6 * rb * d * 4),
        name="moe_gather",
    )(tok_buf, nvalid, x)


def _moe_gu_body(be_ref, nu_ref, x_ref, wg_ref, wl_ref, bg_ref, bl_ref, h_ref):
    used = pl.program_id(1) < nu_ref[0]

    @pl.when(used)
    def _():
        x = x_ref[...]
        gate = jnp.dot(x, wg_ref[...].astype(BF16), preferred_element_type=F32) + bg_ref[...]
        lin = jnp.dot(x, wl_ref[...].astype(BF16), preferred_element_type=F32) + bl_ref[...]
        gate = jnp.minimum(gate, SWIGLU_LIMIT)
        lin = jnp.clip(lin, -SWIGLU_LIMIT, SWIGLU_LIMIT)
        h_ref[...] = (gate * _sigmoid(SWIGLU_ALPHA * gate) * (lin + 1.0)).astype(h_ref.dtype)

    @pl.when(jnp.logical_not(used))
    def _():
        h_ref[...] = jnp.zeros_like(h_ref)


def _moe_gu(xs, block_e, n_used, w_gu, b_gu, layer, rb):
    cap, d = xs.shape
    de = w_gu.shape[3] // 2
    tn = _tile(de, 512, LANES)
    nj = de // tn
    wmap = lambda off: (lambda j, i, be, nu: (layer, be[i], 0, off + j))
    grid_spec = pltpu.PrefetchScalarGridSpec(
        num_scalar_prefetch=2,
        grid=(nj, cap // rb),
        in_specs=[pl.BlockSpec((rb, d), lambda j, i, be, nu: (i, 0)),
                  pl.BlockSpec((None, None, d, tn), wmap(0)),
                  pl.BlockSpec((None, None, d, tn), wmap(nj)),
                  pl.BlockSpec((None, None, 1, tn), wmap(0)),
                  pl.BlockSpec((None, None, 1, tn), wmap(nj))],
        out_specs=pl.BlockSpec((rb, tn), lambda j, i, be, nu: (i, j)),
    )
    b4 = b_gu.reshape(b_gu.shape[0], b_gu.shape[1], 1, b_gu.shape[2])
    return pl.pallas_call(
        _moe_gu_body,
        grid_spec=grid_spec,
        out_shape=jax.ShapeDtypeStruct((cap, de), BF16),
        compiler_params=_cparams(2, 2 * (2 * d * tn * 4) + 2 * d * tn * 2 + 4 * rb * d + 8 * rb * tn * 4 + (4 << 20)),
        name="moe_gu",
    )(block_e, n_used, xs, w_gu, w_gu, b4, b4)


def _moe_down_body(be_ref, nu_ref, h_ref, w_ref, b_ref, y_ref):
    used = pl.program_id(1) < nu_ref[0]

    @pl.when(used)
    def _():
        y_ref[...] = jnp.dot(h_ref[...], w_ref[...].astype(BF16), preferred_element_type=F32) + b_ref[...]

    @pl.when(jnp.logical_not(used))
    def _():
        y_ref[...] = jnp.zeros_like(y_ref)


def _moe_down(h, block_e, n_used, w_down, b_down, layer, rb):
    cap, de = h.shape
    d = w_down.shape[3]
    tn = _tile(d, 1024, LANES)
    wmap = lambda j, i, be, nu: (layer, be[i], 0, j)
    grid_spec = pltpu.PrefetchScalarGridSpec(
        num_scalar_prefetch=2,
        grid=(d // tn, cap // rb),
        in_specs=[pl.BlockSpec((rb, de), lambda j, i, be, nu: (i, 0)),
                  pl.BlockSpec((None, None, de, tn), wmap),
                  pl.BlockSpec((None, None, 1, tn), wmap)],
        out_specs=pl.BlockSpec((rb, tn), lambda j, i, be, nu: (i, j)),
    )
    b4 = b_down.reshape(b_down.shape[0], b_down.shape[1], 1, b_down.shape[2])
    return pl.pallas_call(
        _moe_down_body,
        grid_spec=grid_spec,
        out_shape=jax.ShapeDtypeStruct((cap, d), F32),
        compiler_params=_cparams(2, 2 * de * tn * 4 + de * tn * 2 + 4 * rb * de + 6 * rb * tn * 4 + (4 << 20)),
        name="moe_down",
    )(block_e, n_used, h, w_down, b4)


def _combine_body(pos_ref, y_hbm, x_ref, gt_ref, g_ref, b_ref, xo_ref, xb_ref, buf, sem, *, tb, alpha):
    i = pl.program_id(0)
    n_rows = tb * TOP_K

    def row_copy(blk, slot, t, k):
        p = pos_ref[blk * n_rows + t * TOP_K + k]
        return pltpu.make_async_copy(y_hbm.at[pl.ds(p, 1), :], buf.at[slot, k, pl.ds(t, 1), :], sem.at[slot])

    def issue(blk, slot):
        def start(t, c):
            for k in range(TOP_K):
                row_copy(blk, slot, t, k).start()
            return c

        lax.fori_loop(0, tb, start, 0, unroll=DMA_UNROLL // TOP_K)

    def wait(blk, slot):
        def wait_token(t, c):
            for k in range(TOP_K):
                row_copy(blk, slot, t, k).wait()
            return c

        lax.fori_loop(0, tb, wait_token, 0, unroll=DMA_UNROLL // TOP_K)

    slot = _double_buffered(i, pl.num_programs(0), issue, wait)
    gates = gt_ref[...]
    f = buf[slot, 0] * gates[:, 0:1]
    for k in range(1, TOP_K):
        f = f + buf[slot, k] * gates[:, k:k + 1]
    xn = _layer_norm(alpha * x_ref[...] + f, g_ref[...], b_ref[...])
    xo_ref[...] = xn
    xb_ref[...] = xn.astype(BF16)


def _moe_combine_ln(y_sorted, pos, gates, x, g, b, alpha):
    t, d = x.shape
    tb = _tile(t, 64, SUBLANES)
    row = lambda i, p: (i, 0)
    grid_spec = pltpu.PrefetchScalarGridSpec(
        num_scalar_prefetch=1,
        grid=(t // tb,),
        in_specs=[pl.BlockSpec(memory_space=pl.ANY),
                  pl.BlockSpec((tb, d), row),
                  pl.BlockSpec((tb, LANES), row),
                  pl.BlockSpec((1, d), lambda i, p: (0, 0)),
                  pl.BlockSpec((1, d), lambda i, p: (0, 0))],
        out_specs=[pl.BlockSpec((tb, d), row), pl.BlockSpec((tb, d), row)],
        scratch_shapes=[pltpu.VMEM((2, TOP_K, tb, d), F32), pltpu.SemaphoreType.DMA((2,))],
    )
    return pl.pallas_call(
        functools.partial(_combine_body, tb=tb, alpha=alpha),
        grid_spec=grid_spec,
        out_shape=[jax.ShapeDtypeStruct((t, d), F32), jax.ShapeDtypeStruct((t, d), BF16)],
        compiler_params=_cparams(1, (2 * TOP_K + 10) * tb * d * 4),
        name="moe_combine_ln",
    )(pos, y_sorted, x, gates, g, b)


def _slot_layout(e_idx, rank, counts, rb, gb):
    n_tok = e_idx.shape[0]
    n_experts = counts.shape[0]
    n_assign = n_tok * TOP_K
    padded = (counts + rb - 1) // rb * rb
    pad_ends = jnp.cumsum(padded)
    pad_starts = pad_ends - padded
    pos = (pad_starts[e_idx] + rank).reshape(-1).astype(jnp.int32)
    n_blocks = -(-n_assign // rb) + n_experts
    cap = n_blocks * rb
    tok_buf = jnp.zeros((cap,), jnp.int32).at[pos].set(jnp.arange(n_assign, dtype=jnp.int32) // TOP_K)
    blk0 = jnp.arange(n_blocks, dtype=jnp.int32) * rb
    block_e = jnp.minimum(jnp.searchsorted(pad_ends, blk0, side='right'), n_experts - 1).astype(jnp.int32)
    g0 = jnp.arange(cap // gb, dtype=jnp.int32) * gb
    ge = jnp.repeat(block_e, rb // gb)
    nvalid = jnp.clip(pad_starts[ge] + counts[ge] - g0, 0, gb)
    nvalid = jnp.where(g0 < pad_ends[-1], nvalid, 0).astype(jnp.int32)
    n_used = (pad_ends[-1] // rb).astype(jnp.int32).reshape(1)
    return pos, tok_buf, block_e, nvalid, n_used


def _last_rows(a, row0, n_seq, seq_len, n, width):
    t, cols = a.shape
    assert seq_len % SUBLANES == 0 and t % SUBLANES == 0 and row0 % SUBLANES == 0 and n <= SUBLANES
    step = seq_len // SUBLANES
    first = row0 // SUBLANES + step - 1
    a3 = a.reshape(t // SUBLANES, SUBLANES, cols)
    return lax.slice(a3, (first, SUBLANES - n, 0), (first + (n_seq - 1) * step + 1, SUBLANES, width), (step, 1, 1))


def kernel(x_prompt, x_sample, state_s5_re, state_s5_im, state_gdn, state_gdn_conv, ln_mix_g, ln_mix_b, ln_ffn_g, ln_ffn_b, s5_a_re, s5_a_im, s5_log_step, s5_b_re, s5_b_im, s5_c_re, s5_c_im, s5_d, s5_w_val, s5_w_gate, gdn_w_in, gdn_conv_w, gdn_a_log, gdn_dt_bias, gdn_norm_w, gdn_w_out, moe_router_w, moe_router_b, moe_w_gu, moe_b_gu, moe_w_down, moe_b_down):
    nbp, sp, d = x_prompt.shape
    nbs, ss, _ = x_sample.shape
    depth = ln_mix_g.shape[0]
    n_p = nbp * sp
    n_s = nbs * ss
    n_experts = moe_router_w.shape[2]
    alpha = (2 * depth) ** 0.25
    gp = state_s5_re.shape[2] * state_s5_re.shape[3]
    n_vh, dk, dv = state_gdn.shape[2], state_gdn.shape[3], state_gdn.shape[4]
    conv_dim = state_gdn_conv.shape[3]
    v_dim = n_vh * dv
    n_tail = state_gdn_conv.shape[2]
    rb = MOE_ROW_BLOCK
    gb = MOE_GATHER_BLOCK

    x_parts = [x_prompt.reshape(n_p, d), x_sample.reshape(n_s, d)]
    xb = None

    def group_rows(group):
        if len(x_parts) == 2:
            return x_parts[group], 0
        return x_parts[0], (0, n_p)[group]

    s5_h0r = state_s5_re.reshape(state_s5_re.shape[0], nbs, gp)
    s5_h0i = state_s5_im.reshape(state_s5_im.shape[0], nbs, gp)
    rw_pad = jnp.pad(moe_router_w, ((0, 0), (0, 0), (0, LANES - n_experts))).astype(BF16)
    rb_pad = jnp.pad(moe_router_b, ((0, 0), (0, LANES - n_experts))).reshape(depth, 1, LANES)

    s5_re_p, s5_im_p, s5_re_s, s5_im_s = [], [], [], []
    gdn_p, conv_p, gdn_s, conv_s = [], [], [], []
    for i in range(depth):
        j = i // 2
        if i % 2 == 0:
            prm = _s5_prepare(s5_a_re[j], s5_a_im[j], s5_log_step[j], s5_b_re[j], s5_b_im[j], s5_c_re[j], s5_c_im[j])
            zeros = jnp.zeros((nbp, gp), F32)
            z, hr, hi = _s5_scan(*group_rows(0), 0, nbp, sp, prm, s5_d[j], zeros, zeros,
                                 jnp.zeros((n_p + n_s, d), BF16))
            s5_re_p.append(hr)
            s5_im_p.append(hi)
            z, hr, hi = _s5_scan(*group_rows(1), n_p, nbs, ss, prm, s5_d[j], s5_h0r[j], s5_h0i[j], z)
            s5_re_s.append(hr)
            s5_im_s.append(hi)
            mix = _matmul(z, [s5_w_val, s5_w_gate], j, 0, d, _gated_epilogue, F32, tn_target=256, name="s5_out")
        else:
            if xb is None:
                xb = jnp.concatenate(x_parts, axis=0).astype(BF16)
            proj = _matmul(xb, [gdn_w_in], j, 0, conv_dim + v_dim, _identity_epilogue, F32, name="gdn_in")
            ba = _matmul(xb, [gdn_w_in], j, conv_dim + v_dim, 2 * n_vh, _identity_epilogue, F32, name="gdn_in_ba")
            pad = jnp.zeros((n_vh,), F32)
            acoef = jnp.concatenate([pad, -jnp.exp(gdn_a_log[j])]).reshape(1, LANES)
            dtb = jnp.concatenate([pad, gdn_dt_bias[j]]).reshape(1, LANES)
            nw = gdn_norm_w[j].reshape(1, dv)
            c_zero = jnp.zeros((1, nbp, n_tail, conv_dim), F32)
            s_zero = jnp.zeros((1, nbp, n_vh, dk, dv), F32)
            o, s_new = _gdn_core(proj, ba, 0, nbp, sp, gdn_conv_w[j], c_zero, 0, acoef, dtb, nw, s_zero, 0,
                                 jnp.zeros((n_p + n_s, v_dim), BF16))
            gdn_p.append(s_new)
            o, s_new = _gdn_core(proj, ba, n_p, nbs, ss, gdn_conv_w[j], state_gdn_conv, j, acoef, dtb, nw,
                                 state_gdn, j, o)
            gdn_s.append(s_new)
            conv_p.append(_last_rows(proj, 0, nbp, sp, n_tail, conv_dim))
            conv_s.append(_last_rows(proj, n_p, nbs, ss, n_tail, conv_dim))
            mix = _matmul(o, [gdn_w_out], j, 0, d, _identity_epilogue, F32, tn_target=256, tm_target=512,
                          name="gdn_out")
        x, e_out, r_out, g_out, cnt = _ln_router(x_parts, mix, ln_mix_g[i].reshape(1, d), ln_mix_b[i].reshape(1, d),
                                                 rw_pad[i], rb_pad[i], alpha, n_experts)
        pos, tok_buf, block_e, nvalid, n_used = _slot_layout(
            e_out[:, :TOP_K], r_out[:, :TOP_K], cnt[0, :n_experts].astype(jnp.int32), rb, gb)
        xs = _moe_gather(x, tok_buf, nvalid, gb)
        h = _moe_gu(xs, block_e, n_used, moe_w_gu, moe_b_gu, i, rb)
        y = _moe_down(h, block_e, n_used, moe_w_down, moe_b_down, i, rb)
        x, xb = _moe_combine_ln(y, pos, g_out, x, ln_ffn_g[i].reshape(1, d), ln_ffn_b[i].reshape(1, d), alpha)
        x_parts = [x]

    x = x_parts[0] if len(x_parts) == 1 else jnp.concatenate(x_parts, axis=0)
    gs = state_s5_re.shape[2:]
    st = lambda lst, nb: jnp.stack(lst).reshape((len(lst), nb) + gs)
    return (x[:n_p].reshape(nbp, sp, d), x[n_p:].reshape(nbs, ss, d),
            st(s5_re_p, nbp), st(s5_im_p, nbp), jnp.stack(gdn_p), jnp.stack(conv_p),
            st(s5_re_s, nbs), st(s5_im_s, nbs), jnp.stack(gdn_s), jnp.stack(conv_s))
```

```python
import functools
import math

import jax
import jax.numpy as jnp
from jax import lax
from jax.experimental import pallas as pl
from jax.experimental.pallas import tpu as pltpu

F32 = jnp.float32
BF16 = jnp.bfloat16

TOP_K = 4
SWIGLU_ALPHA = 1.702
SWIGLU_LIMIT = 7.0
LN_EPS = 1e-5
RMS_EPS = 1e-6
L2_EPS = 1e-6

LANES = 128
SUBLANES = 8
VMEM_CAP = 56 << 20

S5_GROUP_BLOCK = 16
GDN_CHUNK = 64
GDN_INV_BLOCK = 16
MOE_ROW_BLOCK = 512
MOE_GATHER_BLOCK = 256
DMA_UNROLL = 8


def _tile(dim, target, align):
    t = min(target, dim) // align * align
    while t >= align:
        if dim % t == 0:
            return t
        t -= align
    return dim


def _cparams(n_axes, vmem_bytes):
    return pltpu.CompilerParams(
        dimension_semantics=("arbitrary",) * n_axes,
        vmem_limit_bytes=int(min(max(vmem_bytes, 16 << 20), VMEM_CAP)))


def _sigmoid(x):
    return 1.0 / (1.0 + jnp.exp(-x))


def _silu(x):
    return x * _sigmoid(x)


def _layer_norm(y, g, b):
    mu = jnp.mean(y, axis=-1, keepdims=True)
    yc = y - mu
    var = jnp.mean(yc * yc, axis=-1, keepdims=True)
    return yc * lax.rsqrt(var + LN_EPS) * g + b


def _mm_body(x_ref, *refs, n_w, epilogue):
    w_refs = refs[:n_w]
    o_ref = refs[n_w]
    wb_refs = refs[n_w + 1:]

    @pl.when(pl.program_id(1) == 0)
    def _():
        for w, wb in zip(w_refs, wb_refs):
            wb[...] = w[...].astype(BF16)

    x = x_ref[...]
    accs = [jnp.dot(x, wb[...], preferred_element_type=F32) for wb in wb_refs]
    o_ref[...] = epilogue(*accs).astype(o_ref.dtype)


def _matmul(x, ws, layer, col0, n_cols, epilogue, out_dtype, tn_target=512, tm_target=1024, name="mm"):
    m, k = x.shape
    tn = _tile(n_cols, tn_target, LANES)
    tm = _tile(m, tm_target, SUBLANES)
    assert col0 % tn == 0
    c0 = col0 // tn
    n_w = len(ws)
    w_spec = pl.BlockSpec((None, k, tn), lambda j, i: (layer, 0, c0 + j))
    vmem = 2 * tm * k * 2 + n_w * (2 * k * tn * 4 + k * tn * 2) + 2 * tm * tn * 4 + (n_w + 1) * tm * tn * 4
    return pl.pallas_call(
        functools.partial(_mm_body, n_w=n_w, epilogue=epilogue),
        grid=(n_cols // tn, m // tm),
        in_specs=[pl.BlockSpec((tm, k), lambda j, i: (i, 0))] + [w_spec] * n_w,
        out_specs=pl.BlockSpec((tm, tn), lambda j, i: (i, j)),
        out_shape=jax.ShapeDtypeStruct((m, n_cols), out_dtype),
        scratch_shapes=[pltpu.VMEM((k, tn), BF16)] * n_w,
        compiler_params=_cparams(2, vmem + (4 << 20)),
        name=name,
    )(x, *ws)


def _gated_epilogue(val, gate):
    return val * _sigmoid(gate)


def _identity_epilogue(acc):
    return acc


def _ln_router_body(*refs, alpha, n_experts, first_blocks):
    x_refs = refs[:len(first_blocks)]
    m_ref, g_ref, b_ref, rw_ref, rb_ref, xo_ref, eo_ref, ro_ref, go_ref, cnt_ref = refs[len(first_blocks):]
    x = x_refs[0][...]
    for p in range(1, len(x_refs)):
        x = jnp.where(pl.program_id(0) >= first_blocks[p], x_refs[p][...], x)
    xn = _layer_norm(alpha * x + m_ref[...], g_ref[...], b_ref[...])
    xo_ref[...] = xn
    tm = xn.shape[0]
    lg = jnp.dot(xn.astype(BF16), rw_ref[...], preferred_element_type=F32) + rb_ref[...]
    lane = lax.broadcasted_iota(jnp.int32, (tm, LANES), 1)
    lane_f = lane.astype(F32)
    lg = jnp.where(lane < n_experts, lg, -jnp.inf)

    @pl.when(pl.program_id(0) == 0)
    def _():
        cnt_ref[...] = jnp.zeros_like(cnt_ref)

    vals, hots = [], []
    e_out = jnp.zeros((tm, LANES), jnp.int32)
    for k in range(TOP_K):
        m = jnp.max(lg, axis=1, keepdims=True)
        idx = jnp.min(jnp.where(lg == m, lane_f, float(LANES)), axis=1, keepdims=True)
        hot = lane_f == idx
        vals.append(m)
        hots.append(hot)
        lg = jnp.where(hot, -jnp.inf, lg)
        e_out = jnp.where(lane == k, idx.astype(jnp.int32), e_out)
    eo_ref[...] = e_out

    ex = [jnp.exp(v - vals[0]) for v in vals]
    den = ex[0]
    for e in ex[1:]:
        den = den + e
    g_out = jnp.zeros((tm, LANES), F32)
    for k in range(TOP_K):
        g_out = jnp.where(lane == k, ex[k] / den, g_out)
    go_ref[...] = g_out

    hot_all = hots[0].astype(F32)
    for h in hots[1:]:
        hot_all = hot_all + h.astype(F32)
    ri = lax.broadcasted_iota(jnp.int32, (tm, tm), 0)
    ci = lax.broadcasted_iota(jnp.int32, (tm, tm), 1)
    before = jnp.dot((ri > ci).astype(BF16), hot_all.astype(BF16), preferred_element_type=F32) + cnt_ref[...]
    r_out = jnp.zeros((tm, LANES), jnp.int32)
    for k in range(TOP_K):
        rank = jnp.sum(jnp.where(hots[k], before, 0.0), axis=1, keepdims=True)
        r_out = jnp.where(lane == k, rank.astype(jnp.int32), r_out)
    ro_ref[...] = r_out
    cnt_ref[...] += jnp.sum(hot_all, axis=0, keepdims=True)


def _ln_router(xs, m, g, b, rw, rb, alpha, n_experts):
    t, d = m.shape
    tm = _tile(math.gcd(*[a.shape[0] for a in xs]), 256, SUBLANES)
    first_blocks, x_specs, blk = [], [], 0
    for a in xs:
        nb = a.shape[0] // tm
        first_blocks.append(blk)
        x_specs.append(pl.BlockSpec((tm, d), lambda i, blk=blk, nb=nb: (jnp.clip(i - blk, 0, nb - 1), 0)))
        blk += nb
    assert blk * tm == t
    row = pl.BlockSpec((tm, d), lambda i: (i, 0))
    vec = pl.BlockSpec((1, d), lambda i: (0, 0))
    small = pl.BlockSpec((tm, LANES), lambda i: (i, 0))
    one = pl.BlockSpec((1, LANES), lambda i: (0, 0))
    return pl.pallas_call(
        functools.partial(_ln_router_body, alpha=alpha, n_experts=n_experts, first_blocks=tuple(first_blocks)),
        grid=(t // tm,),
        in_specs=x_specs + [row, vec, vec, pl.BlockSpec((d, LANES), lambda i: (0, 0)), one],
        out_specs=[row, small, small, small, one],
        out_shape=[jax.ShapeDtypeStruct((t, d), F32), jax.ShapeDtypeStruct((t, LANES), jnp.int32),
                   jax.ShapeDtypeStruct((t, LANES), jnp.int32), jax.ShapeDtypeStruct((t, LANES), F32),
                   jax.ShapeDtypeStruct((1, LANES), F32)],
        compiler_params=_cparams(1, (8 + 2 * len(xs)) * tm * d * 4 + 4 * d * LANES * 4),
        name="ln_router",
    )(*xs, m, g, b, rw, rb)


def _s5_body(x_ref, wb_ref, wc_ref, tab_ref, d_ref, h0r_ref, h0i_ref,
             z_ref, hlr_ref, hli_ref, hre, him, car, *, rows, seq_len, nc, time_sliced):
    u = x_ref[...]
    bu = jnp.dot(u.astype(BF16), wb_ref[...], preferred_element_type=F32)
    hre[...] = bu[:, :nc]
    him[...] = bu[:, nc:]

    def scan8(i, carry):
        sl = pl.ds(pl.multiple_of(i * SUBLANES, SUBLANES), SUBLANES)
        xr = hre[sl, :]
        xi = him[sl, :]
        for n, d in enumerate((1, 2, 4)):
            ar = tab_ref[2 * n]
            ai = tab_ref[2 * n + 1]
            sr = pltpu.roll(xr, d, 0)
            si = pltpu.roll(xi, d, 0)
            xr, xi = xr + (ar * sr - ai * si), xi + (ar * si + ai * sr)
        pr = tab_ref[6]
        pi_ = tab_ref[7]
        cr, ci = carry
        xr = xr + (pr * cr - pi_ * ci)
        xi = xi + (pr * ci + pi_ * cr)
        hre[sl, :] = xr
        him[sl, :] = xi
        return xr[SUBLANES - 1:SUBLANES, :], xi[SUBLANES - 1:SUBLANES, :]

    if time_sliced:
        b = pl.program_id(1)
        t = pl.program_id(2)

        @pl.when(t == 0)
        def _():
            car[0:1, :] = h0r_ref[pl.ds(b, 1), :]
            car[1:2, :] = h0i_ref[pl.ds(b, 1), :]

        cr, ci = lax.fori_loop(0, rows // SUBLANES, scan8, (car[0:1, :], car[1:2, :]))
        car[0:1, :] = cr
        car[1:2, :] = ci

        @pl.when(t == pl.num_programs(2) - 1)
        def _():
            hlr_ref[pl.ds(b, 1), :] = cr
            hli_ref[pl.ds(b, 1), :] = ci
    else:
        per_seq = seq_len // SUBLANES

        def one_seq(s, _):
            carry = (h0r_ref[pl.ds(s, 1), :], h0i_ref[pl.ds(s, 1), :])
            cr, ci = lax.fori_loop(s * per_seq, (s + 1) * per_seq, scan8, carry)
            hlr_ref[pl.ds(s, 1), :] = cr
            hli_ref[pl.ds(s, 1), :] = ci
            return 0

        lax.fori_loop(0, rows // seq_len, one_seq, 0)

    hcat = jnp.concatenate([hre[...].astype(BF16), him[...].astype(BF16)], axis=1)
    y = jnp.dot(hcat, wc_ref[...], preferred_element_type=F32) + d_ref[...] * u
    z = 0.5 * y * (1.0 + lax.erf(y * (1.0 / math.sqrt(2.0))))
    z_ref[...] = z.astype(z_ref.dtype)


def _s5_prepare(a_re, a_im, log_step, b_re, b_im, c_re, c_im):
    g, p = a_re.shape
    cg = b_re.shape[-1]
    gb = min(S5_GROUP_BLOCK, g)
    nblk = g // gb
    step = jnp.exp(log_step)
    zr, zi = a_re * step, a_im * step

    def power(k):
        mag = jnp.exp(k * zr)
        return mag * jnp.cos(k * zi), mag * jnp.sin(k * zi)

    lbr, lbi = power(1.0)
    den = a_re * a_re + a_im * a_im
    nr, ni = lbr - 1.0, lbi
    cf_r = (nr * a_re + ni * a_im) / den
    cf_i = (ni * a_re - nr * a_im) / den
    bb_r = cf_r[..., None] * b_re - cf_i[..., None] * b_im
    bb_i = cf_r[..., None] * b_im + cf_i[..., None] * b_re
    eye = jnp.eye(gb, dtype=F32)

    def blockdiag_in(m):
        m = m.reshape(nblk, gb, p, cg)
        return jnp.einsum('agpc,gh->agchp', m, eye).reshape(nblk, gb * cg, gb * p)

    def blockdiag_out(m):
        m = m.reshape(nblk, gb, cg, p)
        return jnp.einsum('agcp,gh->agphc', m, eye).reshape(nblk, gb * p, gb * cg)

    wb = jnp.concatenate([blockdiag_in(bb_r), blockdiag_in(bb_i)], axis=2).astype(BF16)
    wc = jnp.concatenate([blockdiag_out(c_re), blockdiag_out(-c_im)], axis=1).astype(BF16)

    rows = jnp.arange(SUBLANES, dtype=F32)[:, None]
    tabs = []
    for d in (1, 2, 4):
        pr, pi_ = power(float(d))
        keep = (rows >= d).astype(F32)
        tabs += [keep * pr.reshape(nblk, 1, gb * p), keep * pi_.reshape(nblk, 1, gb * p)]
    kk = (rows + 1.0)[None]
    mag = jnp.exp(kk * zr.reshape(nblk, 1, gb * p))
    ang = kk * zi.reshape(nblk, 1, gb * p)
    tabs += [mag * jnp.cos(ang), mag * jnp.sin(ang)]
    tab = jnp.stack(tabs, axis=1)
    return wb, wc, tab


def _s5_scan(x_src, x_row0, row0, n_seq, seq_len, prm, d_skip, h0r, h0i, z_prev):
    wb, wc, tab = prm
    t_all, d = z_prev.shape
    nblk, kin, n2 = wb.shape
    nc = n2 // 2
    time_sliced = seq_len >= 512
    if time_sliced:
        rows = _tile(seq_len, 512, SUBLANES)
        n_t = seq_len // rows
        grid = (nblk, n_seq, n_t)
        nb_blk = n_seq
        row_map = lambda base: (lambda g, b, t: (base // rows + b * n_t + t, g))
        st_map = lambda g, b, t: (0, g)
    else:
        seqs = _tile(n_seq, max(1024 // seq_len, 1), SUBLANES)
        rows = seqs * seq_len
        grid = (nblk, n_seq // seqs, 1)
        nb_blk = seqs
        row_map = lambda base: (lambda g, b, t: (base // rows + b, g))
        st_map = lambda g, b, t: (b, g)
    assert row0 % rows == 0 and x_row0 % rows == 0
    x_spec = pl.BlockSpec((rows, kin), row_map(x_row0))
    z_spec = pl.BlockSpec((rows, kin), row_map(row0))
    st_spec = pl.BlockSpec((nb_blk, nc), st_map)
    body = functools.partial(_s5_body, rows=rows, seq_len=seq_len, nc=nc, time_sliced=time_sliced)
    in_specs = [x_spec,
                pl.BlockSpec((None, kin, n2), lambda g, b, t: (g, 0, 0)),
                pl.BlockSpec((None, n2, kin), lambda g, b, t: (g, 0, 0)),
                pl.BlockSpec((None, 8, SUBLANES, nc), lambda g, b, t: (g, 0, 0, 0)),
                pl.BlockSpec((1, kin), lambda g, b, t: (0, g)),
                st_spec, st_spec]
    args = [x_src, wb, wc, tab, d_skip.reshape(1, d), h0r, h0i]
    n_in = len(args)
    in_specs.append(pl.BlockSpec(memory_space=pl.ANY))
    args.append(z_prev)
    return pl.pallas_call(
        lambda *refs: body(*refs[:n_in], *refs[n_in + 1:]),
        grid=grid,
        in_specs=in_specs,
        out_specs=[z_spec, st_spec, st_spec],
        out_shape=[jax.ShapeDtypeStruct((t_all, d), BF16),
                   jax.ShapeDtypeStruct(h0r.shape, F32), jax.ShapeDtypeStruct(h0i.shape, F32)],
        scratch_shapes=[pltpu.VMEM((rows, nc), F32), pltpu.VMEM((rows, nc), F32), pltpu.VMEM((SUBLANES, nc), F32)],
        input_output_aliases={n_in: 0},
        compiler_params=_cparams(3, 8 * rows * n2 * 4 + 8 * kin * n2 * 2),
        name="s5_scan",
    )(*args)


def _bdot(a, b):
    return jnp.dot(a.astype(BF16), b.astype(BF16), preferred_element_type=F32)


def _select_dot(sel, x, dims):
    hi = x.astype(BF16)
    r1 = x - hi.astype(F32)
    mid = r1.astype(BF16)
    lo = (r1 - mid.astype(F32)).astype(BF16)
    out = lax.dot_general(sel, hi, dims, preferred_element_type=F32)
    out = out + lax.dot_general(sel, mid, dims, preferred_element_type=F32)
    return out + lax.dot_general(sel, lo, dims, preferred_element_type=F32)


def _unit_lower_inverses(lows, ri, ci):
    c = lows[0].shape[0]
    blk = min(GDN_INV_BLOCK, c)
    eye = (ri == ci).astype(F32)
    diag = ri // blk == ci // blk
    a_s = [jnp.where(diag, -low, 0.0) for low in lows]
    t_s = [eye + a for a in a_s]
    k = 1
    while 2 * k < blk:
        a_s = [_bdot(a, a) for a in a_s]
        t_s = [t + _bdot(t, a) for t, a in zip(t_s, a_s)]
        k *= 2
    while blk < c:
        pair = (ri // (2 * blk) == ci // (2 * blk)) & (ri // blk != ci // blk)
        mids = [_bdot(jnp.where(pair, low, 0.0), t) for low, t in zip(lows, t_s)]
        t_s = [t - _bdot(t, m) for t, m in zip(t_s, mids)]
        blk *= 2
    return t_s


def _gdn_body(q_ref, k_ref, v_ref, z_ref, ba_ref, cwq_ref, cwk_ref, cwv_ref, c0q_ref, c0k_ref, c0v_ref,
              acoef_ref, dtb_ref, nw_ref, s0_ref, o_ref, sn_ref,
              xq, xk, xv, qs, ks, vs, *, rows, seg_rows, carried, n_vh, dk, dv):
    hp = pl.program_id(1)
    halo = SUBLANES - 3
    slot = SUBLANES + seg_rows
    n_seg = rows // seg_rows
    c = GDN_CHUNK
    sub = min(seg_rows, c)
    n_sub = c // sub

    def init():
        for s in range(n_seg):
            xq[s * slot + halo:s * slot + SUBLANES, :] = c0q_ref[s]
            xk[s * slot + halo:s * slot + SUBLANES, :] = c0k_ref[s]
            xv[s * slot + halo:s * slot + SUBLANES, :] = c0v_ref[s]
        sn_ref[...] = s0_ref[...]

    if carried:
        pl.when(pl.program_id(2) == 0)(init)
    else:
        init()

    def conv(x_scr, blk_ref, cw_ref, out_ref, post):
        for s in range(n_seg):
            base = s * slot
            x_scr[base + SUBLANES:base + slot, :] = blk_ref[s * seg_rows:(s + 1) * seg_rows, :]
            y = cw_ref[0:1, :] * x_scr[base + halo:base + halo + seg_rows, :]
            for i in range(1, 4):
                y = y + cw_ref[i:i + 1, :] * x_scr[base + halo + i:base + halo + i + seg_rows, :]
            if carried:
                x_scr[base + halo:base + SUBLANES, :] = x_scr[base + seg_rows + halo:base + slot, :]
            out_ref[s * seg_rows:(s + 1) * seg_rows, :] = post(_silu(y))

    def l2n(x):
        return x * lax.rsqrt(jnp.sum(x * x, axis=-1, keepdims=True) + L2_EPS)

    conv(xq, q_ref, cwq_ref, qs, lambda y: l2n(y) * (dk ** -0.5))
    conv(xk, k_ref, cwk_ref, ks, l2n)
    conv(xv, v_ref, cwv_ref, vs, lambda y: y)

    ba = ba_ref[...]
    beta_all = _sigmoid(ba)
    sp = ba + dtb_ref[...]
    g_all = acoef_ref[...] * (jnp.maximum(sp, 0.0) + jnp.log1p(jnp.exp(-jnp.abs(sp))))
    rr = lax.broadcasted_iota(jnp.int32, (rows, rows), 0)
    cc = lax.broadcasted_iota(jnp.int32, (rows, rows), 1)
    run = ((rr // sub == cc // sub) & (rr >= cc)).astype(BF16)
    mm_dims = (((1,), (0,)), ((), ()))
    gcs = _select_dot(run, g_all, mm_dims)

    ri = lax.broadcasted_iota(jnp.int32, (c, c), 0)
    ci = lax.broadcasted_iota(jnp.int32, (c, c), 1)
    same = ri // sub == ci // sub
    causal = same & (ri >= ci)
    strict = same & (ri > ci)
    lane = lax.broadcasted_iota(jnp.int32, (c, LANES), 1)
    sel_rows = lax.broadcasted_iota(jnp.int32, (SUBLANES, LANES), 0)
    sel_lane = lax.broadcasted_iota(jnp.int32, (SUBLANES, LANES), 1)
    nt_dims = (((1,), (1,)), ((), ()))
    tn_dims = (((0,), (0,)), ((), ()))
    n_chunks = rows // c

    chunks = range(n_chunks)
    probs = [(ic, hh) for ic in chunks for hh in range(2)]
    qc = [qs[ic * c:(ic + 1) * c, :] for ic in chunks]
    kc = [ks[ic * c:(ic + 1) * c, :] for ic in chunks]
    gcc = [gcs[ic * c:(ic + 1) * c, :] for ic in chunks]
    bcc = [beta_all[ic * c:(ic + 1) * c, :] for ic in chunks]
    kq = [lax.dot_general(jnp.concatenate([qc[ic], kc[ic]], axis=0).astype(BF16), kc[ic].astype(BF16), nt_dims,
                          preferred_element_type=F32) for ic in chunks]
    pick = ((sel_rows < 2) & (sel_lane == n_vh + 2 * hp + sel_rows)).astype(BF16)
    grows = [_select_dot(pick, gcc[ic], nt_dims) for ic in chunks]
    beta, gcol, decay, lows = {}, {}, {}, []
    for ic, hh in probs:
        beta[ic, hh] = jnp.sum(jnp.where(lane == 2 * hp + hh, bcc[ic], 0.0), axis=1, keepdims=True)
        gcol[ic, hh] = jnp.sum(jnp.where(lane == n_vh + 2 * hp + hh, gcc[ic], 0.0), axis=1, keepdims=True)
        decay[ic, hh] = jnp.exp(jnp.where(causal, gcol[ic, hh] - grows[ic][hh:hh + 1, :], -jnp.inf))
        lows.append(jnp.where(strict, kq[ic][c:, :] * beta[ic, hh] * decay[ic, hh], 0.0))
    tmats = _unit_lower_inverses(lows, ri, ci)
    u, w, qkm, qe = {}, {}, {}, {}
    for (ic, hh), tmat in zip(probs, tmats):
        eg = jnp.exp(gcol[ic, hh])
        vb = vs[ic * c:(ic + 1) * c, hh * dv:(hh + 1) * dv] * beta[ic, hh]
        uw = _bdot(tmat, jnp.concatenate([vb, kc[ic] * (beta[ic, hh] * eg)], axis=1))
        u[ic, hh] = uw[:, :dv]
        w[ic, hh] = uw[:, dv:]
        qkm[ic, hh] = kq[ic][:c, :] * decay[ic, hh]
        qe[ic, hh] = qc[ic] * eg

    state = {hh: sn_ref[0, hh] for hh in range(2)} if carried else None
    for ic in chunks:
        items = [(hh, g) for hh in range(2) for g in range(n_sub)]
        span = {g: (g * sub, (g + 1) * sub) for g in range(n_sub)}
        s_old = {(hh, g): state[hh] if carried else sn_ref[ic * n_sub + g, hh] for hh, g in items}
        wq = {(hh, g): _bdot(jnp.concatenate([w[ic, hh][span[g][0]:span[g][1], :],
                                              qe[ic, hh][span[g][0]:span[g][1], :]], axis=0), s_old[hh, g])
              for hh, g in items}
        vn = {(hh, g): u[ic, hh][span[g][0]:span[g][1], :] - wq[hh, g][:sub, :] for hh, g in items}
        for hh, g in items:
            a0, a1 = span[g]
            g_last = gcol[ic, hh][a1 - 1:a1, :]
            kdec = kc[ic][a0:a1, :] * jnp.exp(g_last - gcol[ic, hh][a0:a1, :])
            s_new = s_old[hh, g] * jnp.exp(g_last) + lax.dot_general(
                kdec.astype(BF16), vn[hh, g].astype(BF16), tn_dims, preferred_element_type=F32)
            if carried:
                state[hh] = s_new
            else:
                sn_ref[ic * n_sub + g, hh] = s_new
        for hh in range(2):
            v_new = vn[hh, 0] if n_sub == 1 else jnp.concatenate([vn[hh, g] for g in range(n_sub)], axis=0)
            q_s = wq[hh, 0][sub:, :] if n_sub == 1 else jnp.concatenate(
                [wq[hh, g][sub:, :] for g in range(n_sub)], axis=0)
            o = q_s + _bdot(qkm[ic, hh], v_new)
            o = o * lax.rsqrt(jnp.mean(o * o, axis=-1, keepdims=True) + RMS_EPS) * nw_ref[...]
            o = o * _silu(z_ref[ic * c:(ic + 1) * c, hh * dv:(hh + 1) * dv])
            o_ref[ic * c:(ic + 1) * c, hh * dv:(hh + 1) * dv] = o.astype(o_ref.dtype)
    if carried:
        for hh in range(2):
            sn_ref[0, hh] = state[hh]


def _gdn_core(proj, ba, row0, n_seq, seq_len, conv_w, conv0, conv_idx, acoef, dtb, norm_w, s0, s0_idx, o_prev,
              s_all, s_out_idx, n_layers):
    t_all = proj.shape[0]
    n_vh, dk, dv = s0.shape[2], s0.shape[3], s0.shape[4]
    n_kh = n_vh // 2
    k_dim = n_kh * dk
    v_dim = n_vh * dv
    carried = seq_len >= GDN_CHUNK
    if carried:
        seg_rows = rows = _tile(seq_len, 8 * GDN_CHUNK, GDN_CHUNK)
        sb = 1
    else:
        per_chunk = GDN_CHUNK // seq_len
        sb = _tile(n_seq, 4 * per_chunk, per_chunk)
        seg_rows = seq_len
        rows = sb * seq_len
    n_t = seq_len // seg_rows
    assert rows % GDN_CHUNK == 0 and row0 % rows == 0 and n_seq % sb == 0
    assert dk == LANES and dv == LANES and 2 * n_vh == LANES and seq_len >= 3
    r0 = row0 // rows
    kb = k_dim // dk
    vb_off = 2 * k_dim // (2 * dv)
    zb_off = (2 * k_dim + v_dim) // (2 * dv)
    rmap = lambda off: (lambda b, h, t: (r0 + b * n_t + t, off + h))
    cmap = lambda off: (lambda b, h, t: (0, off + h))
    c0map = lambda off: (lambda b, h, t: (conv_idx, b, 0, off + h))
    vec = pl.BlockSpec((1, LANES), lambda b, h, t: (0, 0))
    in_specs = [
        pl.BlockSpec((rows, dk), rmap(0)),
        pl.BlockSpec((rows, dk), rmap(kb)),
        pl.BlockSpec((rows, 2 * dv), rmap(vb_off)),
        pl.BlockSpec((rows, 2 * dv), rmap(zb_off)),
        pl.BlockSpec((rows, LANES), lambda b, h, t: (r0 + b * n_t + t, 0)),
        pl.BlockSpec((4, dk), cmap(0)),
        pl.BlockSpec((4, dk), cmap(kb)),
        pl.BlockSpec((4, 2 * dv), cmap(vb_off)),
        pl.BlockSpec((None, sb, 3, dk), c0map(0)),
        pl.BlockSpec((None, sb, 3, dk), c0map(kb)),
        pl.BlockSpec((None, sb, 3, 2 * dv), c0map(vb_off)),
        vec, vec, vec,
        pl.BlockSpec((None, sb, 2, dk, dv), lambda b, h, t: (s0_idx, b, h, 0, 0)),
    ]
    args = [proj, proj, proj, proj, ba, conv_w, conv_w, conv_w, conv0, conv0, conv0, acoef, dtb, norm_w, s0]
    o_spec = pl.BlockSpec((rows, 2 * dv), lambda b, h, t: (r0 + b * n_t + t, h))
    body = functools.partial(_gdn_body, rows=rows, seg_rows=seg_rows, carried=carried, n_vh=n_vh, dk=dk, dv=dv)
    n_in = len(args)
    in_specs.append(pl.BlockSpec(memory_space=pl.ANY))
    args.append(o_prev)
    aliases = {n_in: 0}
    if s_all is not None:
        in_specs.append(pl.BlockSpec(memory_space=pl.ANY))
        args.append(s_all)
        aliases[n_in + 1] = 1
    xrows = (rows // seg_rows) * (SUBLANES + seg_rows)
    return pl.pallas_call(
        lambda *refs: body(*refs[:n_in], *refs[len(args):]),
        grid=(n_seq // sb, n_kh, n_t),
        in_specs=in_specs,
        out_specs=[o_spec, pl.BlockSpec((None, sb, 2, dk, dv), lambda b, h, t: (s_out_idx, b, h, 0, 0))],
        out_shape=[jax.ShapeDtypeStruct((t_all, v_dim), BF16),
                   jax.ShapeDtypeStruct((n_layers, n_seq, n_vh, dk, dv), F32)],
        scratch_shapes=[pltpu.VMEM((xrows, dk), F32), pltpu.VMEM((xrows, dk), F32),
                        pltpu.VMEM((xrows, 2 * dv), F32),
                        pltpu.VMEM((rows, dk), F32), pltpu.VMEM((rows, dk), F32), pltpu.VMEM((rows, 2 * dv), F32)],
        input_output_aliases=aliases,
        compiler_params=_cparams(3, 48 * rows * LANES * 4 + 8 * sb * 2 * dk * dv * 4 + (8 << 20)),
        name="gdn_core",
    )(*args)


def _double_buffered(i, n, issue, wait):
    slot = i % 2

    @pl.when(i == 0)
    def _():
        issue(i, slot)

    @pl.when(i + 1 < n)
    def _():
        issue(i + 1, 1 - slot)

    wait(i, slot)
    return slot


def _gather_body(tok_ref, nval_ref, x_hbm, o_ref, buf, sem, *, rb):
    i = pl.program_id(0)

    def row_copy(blk, slot, r):
        tok = tok_ref[blk * rb + r]
        return pltpu.make_async_copy(x_hbm.at[pl.ds(tok, 1), :], buf.at[slot, pl.ds(r, 1), :], sem.at[slot])

    def issue(blk, slot):
        def start(r, c):
            row_copy(blk, slot, r).start()
            return c

        @pl.when(nval_ref[blk] > 0)
        def _():
            lax.fori_loop(0, rb, start, 0, unroll=DMA_UNROLL)

    def wait(blk, slot):
        def wait_row(r, c):
            row_copy(blk, slot, r).wait()
            return c

        @pl.when(nval_ref[blk] > 0)
        def _():
            lax.fori_loop(0, rb, wait_row, 0, unroll=DMA_UNROLL)

    slot = _double_buffered(i, pl.num_programs(0), issue, wait)

    @pl.when(nval_ref[i] > 0)
    def _():
        o_ref[...] = buf[slot].astype(o_ref.dtype)

    @pl.when(nval_ref[i] == 0)
    def _():
        o_ref[...] = jnp.zeros_like(o_ref)


def _moe_gather(x, tok_buf, nvalid, rb):
    t, d = x.shape
    cap = tok_buf.shape[0]
    grid_spec = pltpu.PrefetchScalarGridSpec(
        num_scalar_prefetch=2,
        grid=(cap // rb,),
        in_specs=[pl.BlockSpec(memory_space=pl.ANY)],
        out_specs=pl.BlockSpec((rb, d), lambda i, tok, nv: (i, 0)),
        scratch_shapes=[pltpu.VMEM((2, rb, d), F32), pltpu.SemaphoreType.DMA((2,))],
    )
    return pl.pallas_call(
        functools.partial(_gather_body, rb=rb),
        grid_spec=grid_spec,
        out_shape=jax.ShapeDtypeStruct((cap, d), BF16),
        compiler_params=_cparams(1, 6 * rb * d * 4),
        name="moe_gather",
    )(tok_buf, nvalid, x)


def _moe_gu_body(be_ref, nu_ref, x_ref, wg_ref, wl_ref, bg_ref, bl_ref, h_ref):
    used = pl.program_id(1) < nu_ref[0]

    @pl.when(used)
    def _():
        x = x_ref[...]
        gate = jnp.dot(x, wg_ref[...].astype(BF16), preferred_element_type=F32) + bg_ref[...]
        lin = jnp.dot(x, wl_ref[...].astype(BF16), preferred_element_type=F32) + bl_ref[...]
        gate = jnp.minimum(gate, SWIGLU_LIMIT)
        lin = jnp.clip(lin, -SWIGLU_LIMIT, SWIGLU_LIMIT)
        h_ref[...] = (gate * _sigmoid(SWIGLU_ALPHA * gate) * (lin + 1.0)).astype(h_ref.dtype)

    @pl.when(jnp.logical_not(used))
    def _():
        h_ref[...] = jnp.zeros_like(h_ref)


def _moe_gu(xs, block_e, n_used, w_gu, b_gu, layer, rb):
    cap, d = xs.shape
    de = w_gu.shape[3] // 2
    tn = _tile(de, 512, LANES)
    nj = de // tn
    wmap = lambda off: (lambda j, i, be, nu: (layer, be[i], 0, off + j))
    grid_spec = pltpu.PrefetchScalarGridSpec(
        num_scalar_prefetch=2,
        grid=(nj, cap // rb),
        in_specs=[pl.BlockSpec((rb, d), lambda j, i, be, nu: (i, 0)),
                  pl.BlockSpec((None, None, d, tn), wmap(0)),
                  pl.BlockSpec((None, None, d, tn), wmap(nj)),
                  pl.BlockSpec((None, None, 1, tn), wmap(0)),
                  pl.BlockSpec((None, None, 1, tn), wmap(nj))],
        out_specs=pl.BlockSpec((rb, tn), lambda j, i, be, nu: (i, j)),
    )
    b4 = b_gu.reshape(b_gu.shape[0], b_gu.shape[1], 1, b_gu.shape[2])
    return pl.pallas_call(
        _moe_gu_body,
        grid_spec=grid_spec,
        out_shape=jax.ShapeDtypeStruct((cap, de), BF16),
        compiler_params=_cparams(2, 2 * (2 * d * tn * 4) + 2 * d * tn * 2 + 4 * rb * d + 8 * rb * tn * 4 + (4 << 20)),
        name="moe_gu",
    )(block_e, n_used, xs, w_gu, w_gu, b4, b4)


def _moe_down_body(be_ref, nu_ref, h_ref, w_ref, b_ref, y_ref):
    used = pl.program_id(1) < nu_ref[0]

    @pl.when(used)
    def _():
        y_ref[...] = jnp.dot(h_ref[...], w_ref[...].astype(BF16), preferred_element_type=F32) + b_ref[...]

    @pl.when(jnp.logical_not(used))
    def _():
        y_ref[...] = jnp.zeros_like(y_ref)


def _moe_down(h, block_e, n_used, w_down, b_down, layer, rb):
    cap, de = h.shape
    d = w_down.shape[3]
    tn = _tile(d, 1024, LANES)
    wmap = lambda j, i, be, nu: (layer, be[i], 0, j)
    grid_spec = pltpu.PrefetchScalarGridSpec(
        num_scalar_prefetch=2,
        grid=(d // tn, cap // rb),
        in_specs=[pl.BlockSpec((rb, de), lambda j, i, be, nu: (i, 0)),
                  pl.BlockSpec((None, None, de, tn), wmap),
                  pl.BlockSpec((None, None, 1, tn), wmap)],
        out_specs=pl.BlockSpec((rb, tn), lambda j, i, be, nu: (i, j)),
    )
    b4 = b_down.reshape(b_down.shape[0], b_down.shape[1], 1, b_down.shape[2])
    return pl.pallas_call(
        _moe_down_body,
        grid_spec=grid_spec,
        out_shape=jax.ShapeDtypeStruct((cap, d), F32),
        compiler_params=_cparams(2, 2 * de * tn * 4 + de * tn * 2 + 4 * rb * de + 6 * rb * tn * 4 + (4 << 20)),
        name="moe_down",
    )(block_e, n_used, h, w_down, b4)


def _combine_body(pos_ref, y_hbm, x_ref, gt_ref, g_ref, b_ref, xo_ref, xb_ref, buf, sem, *, tb, alpha):
    i = pl.program_id(0)
    n_rows = tb * TOP_K

    def row_copy(blk, slot, t, k):
        p = pos_ref[blk * n_rows + t * TOP_K + k]
        return pltpu.make_async_copy(y_hbm.at[pl.ds(p, 1), :], buf.at[slot, k, pl.ds(t, 1), :], sem.at[slot])

    def issue(blk, slot):
        def start(t, c):
            for k in range(TOP_K):
                row_copy(blk, slot, t, k).start()
            return c

        lax.fori_loop(0, tb, start, 0, unroll=DMA_UNROLL // TOP_K)

    def wait(blk, slot):
        def wait_token(t, c):
            for k in range(TOP_K):
                row_copy(blk, slot, t, k).wait()
            return c

        lax.fori_loop(0, tb, wait_token, 0, unroll=DMA_UNROLL // TOP_K)

    slot = _double_buffered(i, pl.num_programs(0), issue, wait)
    gates = gt_ref[...]
    f = buf[slot, 0] * gates[:, 0:1]
    for k in range(1, TOP_K):
        f = f + buf[slot, k] * gates[:, k:k + 1]
    xn = _layer_norm(alpha * x_ref[...] + f, g_ref[...], b_ref[...])
    xo_ref[...] = xn
    xb_ref[...] = xn.astype(BF16)


def _moe_combine_ln(y_sorted, pos, gates, x, g, b, alpha):
    t, d = x.shape
    tb = _tile(t, 64, SUBLANES)
    row = lambda i, p: (i, 0)
    grid_spec = pltpu.PrefetchScalarGridSpec(
        num_scalar_prefetch=1,
        grid=(t // tb,),
        in_specs=[pl.BlockSpec(memory_space=pl.ANY),
                  pl.BlockSpec((tb, d), row),
                  pl.BlockSpec((tb, LANES), row),
                  pl.BlockSpec((1, d), lambda i, p: (0, 0)),
                  pl.BlockSpec((1, d), lambda i, p: (0, 0))],
        out_specs=[pl.BlockSpec((tb, d), row), pl.BlockSpec((tb, d), row)],
        scratch_shapes=[pltpu.VMEM((2, TOP_K, tb, d), F32), pltpu.SemaphoreType.DMA((2,))],
    )
    return pl.pallas_call(
        functools.partial(_combine_body, tb=tb, alpha=alpha),
        grid_spec=grid_spec,
        out_shape=[jax.ShapeDtypeStruct((t, d), F32), jax.ShapeDtypeStruct((t, d), BF16)],
        compiler_params=_cparams(1, (2 * TOP_K + 10) * tb * d * 4),
        name="moe_combine_ln",
    )(pos, y_sorted, x, gates, g, b)


def _slot_layout(e_idx, rank, counts, rb, gb):
    n_tok = e_idx.shape[0]
    n_experts = counts.shape[0]
    n_assign = n_tok * TOP_K
    padded = (counts + rb - 1) // rb * rb
    pad_ends = jnp.cumsum(padded)
    pad_starts = pad_ends - padded
    pos = (pad_starts[e_idx] + rank).reshape(-1).astype(jnp.int32)
    n_blocks = -(-n_assign // rb) + n_experts
    cap = n_blocks * rb
    tok_buf = jnp.zeros((cap,), jnp.int32).at[pos].set(jnp.arange(n_assign, dtype=jnp.int32) // TOP_K)
    blk0 = jnp.arange(n_blocks, dtype=jnp.int32) * rb
    block_e = jnp.minimum(jnp.searchsorted(pad_ends, blk0, side='right'), n_experts - 1).astype(jnp.int32)
    g0 = jnp.arange(cap // gb, dtype=jnp.int32) * gb
    ge = jnp.repeat(block_e, rb // gb)
    nvalid = jnp.clip(pad_starts[ge] + counts[ge] - g0, 0, gb)
    nvalid = jnp.where(g0 < pad_ends[-1], nvalid, 0).astype(jnp.int32)
    n_used = (pad_ends[-1] // rb).astype(jnp.int32).reshape(1)
    return pos, tok_buf, block_e, nvalid, n_used


def _last_rows(a, row0, n_seq, seq_len, n, width):
    t, cols = a.shape
    assert seq_len % SUBLANES == 0 and t % SUBLANES == 0 and row0 % SUBLANES == 0 and n <= SUBLANES
    step = seq_len // SUBLANES
    first = row0 // SUBLANES + step - 1
    a3 = a.reshape(t // SUBLANES, SUBLANES, cols)
    return lax.slice(a3, (first, SUBLANES - n, 0), (first + (n_seq - 1) * step + 1, SUBLANES, width), (step, 1, 1))


def kernel(x_prompt, x_sample, state_s5_re, state_s5_im, state_gdn, state_gdn_conv, ln_mix_g, ln_mix_b, ln_ffn_g, ln_ffn_b, s5_a_re, s5_a_im, s5_log_step, s5_b_re, s5_b_im, s5_c_re, s5_c_im, s5_d, s5_w_val, s5_w_gate, gdn_w_in, gdn_conv_w, gdn_a_log, gdn_dt_bias, gdn_norm_w, gdn_w_out, moe_router_w, moe_router_b, moe_w_gu, moe_b_gu, moe_w_down, moe_b_down):
    nbp, sp, d = x_prompt.shape
    nbs, ss, _ = x_sample.shape
    depth = ln_mix_g.shape[0]
    n_p = nbp * sp
    n_s = nbs * ss
    n_experts = moe_router_w.shape[2]
    alpha = (2 * depth) ** 0.25
    gp = state_s5_re.shape[2] * state_s5_re.shape[3]
    n_vh, dk, dv = state_gdn.shape[2], state_gdn.shape[3], state_gdn.shape[4]
    conv_dim = state_gdn_conv.shape[3]
    v_dim = n_vh * dv
    n_tail = state_gdn_conv.shape[2]
    rb = MOE_ROW_BLOCK
    gb = MOE_GATHER_BLOCK

    x_parts = [x_prompt.reshape(n_p, d), x_sample.reshape(n_s, d)]
    xb = None

    def group_rows(group):
        if len(x_parts) == 2:
            return x_parts[group], 0
        return x_parts[0], (0, n_p)[group]

    s5_h0r = state_s5_re.reshape(state_s5_re.shape[0], nbs, gp)
    s5_h0i = state_s5_im.reshape(state_s5_im.shape[0], nbs, gp)
    rw_pad = jnp.pad(moe_router_w, ((0, 0), (0, 0), (0, LANES - n_experts))).astype(BF16)
    rb_pad = jnp.pad(moe_router_b, ((0, 0), (0, LANES - n_experts))).reshape(depth, 1, LANES)

    s5_re_p, s5_im_p, s5_re_s, s5_im_s = [], [], [], []
    conv_p, conv_s = [], []
    n_gdn = state_gdn.shape[0]
    gdn_p = jnp.zeros((n_gdn, nbp, n_vh, dk, dv), F32)
    gdn_s = None
    for i in range(depth):
        j = i // 2
        if i % 2 == 0:
            prm = _s5_prepare(s5_a_re[j], s5_a_im[j], s5_log_step[j], s5_b_re[j], s5_b_im[j], s5_c_re[j], s5_c_im[j])
            zeros = jnp.zeros((nbp, gp), F32)
            z, hr, hi = _s5_scan(*group_rows(0), 0, nbp, sp, prm, s5_d[j], zeros, zeros,
                                 jnp.zeros((n_p + n_s, d), BF16))
            s5_re_p.append(hr)
            s5_im_p.append(hi)
            z, hr, hi = _s5_scan(*group_rows(1), n_p, nbs, ss, prm, s5_d[j], s5_h0r[j], s5_h0i[j], z)
            s5_re_s.append(hr)
            s5_im_s.append(hi)
            mix = _matmul(z, [s5_w_val, s5_w_gate], j, 0, d, _gated_epilogue, F32, tn_target=256, name="s5_out")
        else:
            if xb is None:
                xb = jnp.concatenate(x_parts, axis=0).astype(BF16)
            proj = _matmul(xb, [gdn_w_in], j, 0, conv_dim + v_dim, _identity_epilogue, F32, name="gdn_in")
            ba = _matmul(xb, [gdn_w_in], j, conv_dim + v_dim, 2 * n_vh, _identity_epilogue, F32, name="gdn_in_ba")
            pad = jnp.zeros((n_vh,), F32)
            acoef = jnp.concatenate([pad, -jnp.exp(gdn_a_log[j])]).reshape(1, LANES)
            dtb = jnp.concatenate([pad, gdn_dt_bias[j]]).reshape(1, LANES)
            nw = gdn_norm_w[j].reshape(1, dv)
            c_zero = jnp.zeros((1, nbp, n_tail, conv_dim), F32)
            s_zero = jnp.zeros((1, nbp, n_vh, dk, dv), F32)
            o, gdn_p = _gdn_core(proj, ba, 0, nbp, sp, gdn_conv_w[j], c_zero, 0, acoef, dtb, nw, s_zero, 0,
                                 jnp.zeros((n_p + n_s, v_dim), BF16), gdn_p, j, n_gdn)
            o, gdn_s = _gdn_core(proj, ba, n_p, nbs, ss, gdn_conv_w[j], state_gdn_conv, j, acoef, dtb, nw,
                                 state_gdn, j, o, gdn_s, j, n_gdn)
            conv_p.append(_last_rows(proj, 0, nbp, sp, n_tail, conv_dim))
            conv_s.append(_last_rows(proj, n_p, nbs, ss, n_tail, conv_dim))
            mix = _matmul(o, [gdn_w_out], j, 0, d, _identity_epilogue, F32, tn_target=256, tm_target=512,
                          name="gdn_out")
        x, e_out, r_out, g_out, cnt = _ln_router(x_parts, mix, ln_mix_g[i].reshape(1, d), ln_mix_b[i].reshape(1, d),
                                                 rw_pad[i], rb_pad[i], alpha, n_experts)
        pos, tok_buf, block_e, nvalid, n_used = _slot_layout(
            e_out[:, :TOP_K], r_out[:, :TOP_K], cnt[0, :n_experts].astype(jnp.int32), rb, gb)
        xs = _moe_gather(x, tok_buf, nvalid, gb)
        h = _moe_gu(xs, block_e, n_used, moe_w_gu, moe_b_gu, i, rb)
        y = _moe_down(h, block_e, n_used, moe_w_down, moe_b_down, i, rb)
        x, xb = _moe_combine_ln(y, pos, g_out, x, ln_ffn_g[i].reshape(1, d), ln_ffn_b[i].reshape(1, d), alpha)
        x_parts = [x]

    x = x_parts[0] if len(x_parts) == 1 else jnp.concatenate(x_parts, axis=0)
    gs = state_s5_re.shape[2:]
    st = lambda lst, nb: jnp.stack(lst).reshape((len(lst), nb) + gs)
    return (x[:n_p].reshape(nbp, sp, d), x[n_p:].reshape(nbs, ss, d),
            st(s5_re_p, nbp), st(s5_im_p, nbp), gdn_p, jnp.stack(conv_p),
            st(s5_re_s, nbs), st(s5_im_s, nbs), gdn_s, jnp.stack(conv_s))
```
